```python
import jax, jax.numpy as jnp
from jax import lax
import numpy as np

D_MODEL = 1024
BATCH = 8
SEQ = 2048
DEPTH = 4
DEC_BATCH = 128
DEC_SEQ = 8
PAST_LEN = 16384
PAGE_SIZE = 128

CONV_DIM = D_MODEL // 4
RET_DIM = D_MODEL // 2
POOL_DIM = D_MODEL // 4
MIX_DIM = CONV_DIM + RET_DIM + POOL_DIM
RET_HEADS = 4
RET_HEAD_DIM = RET_DIM // RET_HEADS
RET_CHUNK = 128
ROPE_BASE = 10000.0
CONV_K = 3
POOL_WINDOWS = (2, 4, 8, 16)
POOL_GROUPS = len(POOL_WINDOWS)
POOL_GROUP_DIM = POOL_DIM // POOL_GROUPS
POOL_BUF = max(POOL_WINDOWS) - 1
D_FF = 2816
EPS = 1e-6
MIX_IN_DIM = 3 * CONV_DIM + 4 * RET_DIM + POOL_DIM

kernel_name = "hymba_conv_retention_pool_macaron_step"


def rms_norm(x, g):
    xf = x.astype(jnp.float32)
    y = xf * lax.rsqrt(jnp.mean(xf * xf, axis=-1, keepdims=True) + EPS)
    return (y * g.astype(jnp.float32)).astype(x.dtype)


def swiglu(x, w_gate, w_up, w_down):
    return (jax.nn.silu(x @ w_gate) * (x @ w_up)) @ w_down


def rope(x, pos):
    half = x.shape[-1] // 2
    inv = ROPE_BASE ** (-jnp.arange(half, dtype=jnp.float32) / half)
    ang = pos.astype(jnp.float32)[:, None] * inv[None, :]
    cos = jnp.cos(ang)[None, :, None, :]
    sin = jnp.sin(ang)[None, :, None, :]
    xf = x.astype(jnp.float32)
    x1, x2 = xf[..., :half], xf[..., half:]
    return jnp.concatenate([x1 * cos - x2 * sin, x1 * sin + x2 * cos], axis=-1)


def retention_chunkwise(q, k, v, S0):
    B, L, H, dk = q.shape
    dv = v.shape[-1]
    C = RET_CHUNK if L % RET_CHUNK == 0 else L
    n = L // C
    lg = jnp.log(1.0 - 2.0 ** (-5.0 - jnp.arange(H, dtype=jnp.float32)))
    idx = jnp.arange(C)
    diff = (idx[:, None] - idx[None, :]).astype(jnp.float32)
    decay_mat = jnp.where(diff[None] >= 0, jnp.exp(jnp.maximum(diff, 0.0)[None] * lg[:, None, None]), 0.0)
    cross_decay = jnp.exp((idx + 1).astype(jnp.float32)[:, None] * lg[None, :])[None, :, :, None]
    k_decay = jnp.exp((C - 1 - idx).astype(jnp.float32)[:, None] * lg[None, :])[None, :, :, None]
    chunk_decay = jnp.exp(C * lg)[None, :, None, None]

    def to_chunks(t):
        return jnp.moveaxis(t.reshape(B, n, C, H, t.shape[-1]), 1, 0)

    def step(S, inp):
        qc, kc, vc = inp
        scores = jnp.einsum('bihd,bjhd->bhij', qc, kc) * decay_mat[None]
        inner = jnp.einsum('bhij,bjhe->bihe', scores, vc)
        cross = jnp.einsum('bihd,bhde->bihe', qc, S) * cross_decay
        S_new = S * chunk_decay + jnp.einsum('bjhd,bjhe->bhde', kc * k_decay, vc)
        return S_new, inner + cross

    S_fin, out = lax.scan(step, S0, (to_chunks(q), to_chunks(k), to_chunks(v)))
    out = jnp.moveaxis(out, 0, 1).reshape(B, L, H, dv)
    return out, S_fin


def retention_mixer(q, k, v, g, S0, pos):
    B, L, _ = q.shape
    shp = (B, L, RET_HEADS, RET_HEAD_DIM)
    qh = rope(q.reshape(shp), pos) * (RET_HEAD_DIM ** -0.5)
    kh = rope(k.reshape(shp), pos)
    vh = v.reshape(shp).astype(jnp.float32)
    out, S = retention_chunkwise(qh, kh, vh, S0.astype(jnp.float32))
    mu = jnp.mean(out, axis=-1, keepdims=True)
    var = jnp.mean(jnp.square(out - mu), axis=-1, keepdims=True)
    out = (out - mu) * lax.rsqrt(var + EPS)
    y = jax.nn.silu(g.astype(jnp.float32)) * out.reshape(B, L, RET_DIM)
    return y.astype(q.dtype), S.astype(S0.dtype)


def short_conv_mixer(u_in, b_gate, c_gate, w, buf):
    L = u_in.shape[1]
    u = c_gate * u_in
    ext = jnp.concatenate([buf.astype(u.dtype), u], axis=1)
    y = ext[:, 0:L] * w[0]
    for t in range(1, CONV_K):
        y = y + ext[:, t:t + L] * w[t]
    return b_gate * y, ext[:, -(CONV_K - 1):]


def pool_mixer(u, buf, pos, w, scale):
    B, L, P = u.shape
    ext = jnp.concatenate([buf.astype(u.dtype), u], axis=1)
    cs = jnp.concatenate([jnp.zeros((B, 1, P), jnp.float32), jnp.cumsum(ext.astype(jnp.float32), axis=1)], axis=1)
    means = []
    for gi, win in enumerate(POOL_WINDOWS):
        sl = slice(gi * POOL_GROUP_DIM, (gi + 1) * POOL_GROUP_DIM)
        s = cs[:, POOL_BUF + 1:POOL_BUF + 1 + L, sl] - cs[:, POOL_BUF + 1 - win:POOL_BUF + 1 - win + L, sl]
        cnt = jnp.minimum(pos + 1, win).astype(jnp.float32)[None, :, None]
        means.append(s / cnt)
    d = (jnp.concatenate(means, axis=-1) - u.astype(jnp.float32)).astype(u.dtype)
    d = d.reshape(B, L, POOL_GROUPS, POOL_GROUP_DIM)
    y = jnp.einsum('blgc,gcd->blgd', d, w).reshape(B, L, P) * scale
    return y, ext[:, -POOL_BUF:]


def trunk(x, pos0, conv_state, ret_state, pool_state, p):
    L = x.shape[1]
    pos = pos0 + jnp.arange(L, dtype=jnp.int32)
    split_at = np.cumsum([CONV_DIM, CONV_DIM, CONV_DIM, RET_DIM, RET_DIM, RET_DIM, RET_DIM])
    new_conv, new_ret, new_pool = [], [], []
    for l in range(DEPTH):
        h = rms_norm(x, p['ffn1_pre_g'][l])
        x = x + 0.5 * rms_norm(swiglu(h, p['ffn1_w_gate'][l], p['ffn1_w_up'][l], p['ffn1_w_down'][l]), p['ffn1_post_g'][l])
        h = rms_norm(x, p['mix_pre_g'][l])
        z = h @ p['w_mix_in'][l]
        cx, cb, cc, q, k, v, g, pu = jnp.split(z, split_at, axis=-1)
        yc, cbuf = short_conv_mixer(cx, cb, cc, p['conv_w'][l], conv_state[l])
        yr, S = retention_mixer(q, k, v, g, ret_state[l], pos)
        yp, pbuf = pool_mixer(pu, pool_state[l], pos, p['pool_w'][l], p['pool_scale'][l])
        m = jnp.concatenate([yc, yr, yp], axis=-1) @ p['w_mix_out'][l]
        x = x + rms_norm(m, p['mix_post_g'][l])
        h = rms_norm(x, p['ffn2_pre_g'][l])
        x = x + 0.5 * rms_norm(swiglu(h, p['ffn2_w_gate'][l], p['ffn2_w_up'][l], p['ffn2_w_down'][l]), p['ffn2_post_g'][l])
        new_conv.append(cbuf)
        new_ret.append(S)
        new_pool.append(pbuf)
    return x, jnp.stack(new_conv), jnp.stack(new_ret), jnp.stack(new_pool)


def setup_inputs(seed: int = 0) -> dict:
    key = jax.random.key(seed)
    ks = jax.random.split(key, 24)
    f32 = jnp.float32

    def nrm(k, shape, scale):
        return jax.random.normal(k, shape, f32) * scale

    def gain(k):
        return 1.0 + 0.1 * jax.random.normal(k, (DEPTH, D_MODEL), f32)

    return {
        "x_prompt": nrm(ks[0], (BATCH, SEQ, D_MODEL), 1.0),
        "x_sample": nrm(ks[1], (DEC_BATCH, DEC_SEQ, D_MODEL), 1.0),
        "state_conv": nrm(ks[2], (DEPTH, DEC_BATCH, CONV_K - 1, CONV_DIM), 1.0),
        "state_ret": nrm(ks[3], (DEPTH, DEC_BATCH, RET_HEADS, RET_HEAD_DIM, RET_HEAD_DIM), 2.0),
        "state_pool": nrm(ks[4], (DEPTH, DEC_BATCH, POOL_BUF, POOL_DIM), 1.0),
        "ffn1_pre_g": gain(ks[5]),
        "ffn1_post_g": gain(ks[6]),
        "ffn1_w_gate": nrm(ks[7], (DEPTH, D_MODEL, D_FF), D_MODEL ** -0.5),
        "ffn1_w_up": nrm(ks[8], (DEPTH, D_MODEL, D_FF), D_MODEL ** -0.5),
        "ffn1_w_down": nrm(ks[9], (DEPTH, D_FF, D_MODEL), D_FF ** -0.5),
        "mix_pre_g": gain(ks[10]),
        "mix_post_g": gain(ks[11]),
        "w_mix_in": nrm(ks[12], (DEPTH, D_MODEL, MIX_IN_DIM), D_MODEL ** -0.5),
        "conv_w": nrm(ks[13], (DEPTH, CONV_K, CONV_DIM), CONV_K ** -0.5),
        "pool_w": nrm(ks[14], (DEPTH, POOL_GROUPS, POOL_GROUP_DIM, POOL_GROUP_DIM), POOL_GROUP_DIM ** -0.5),
        "pool_scale": 1.0 + 0.1 * jax.random.normal(ks[15], (DEPTH, POOL_DIM), f32),
        "w_mix_out": nrm(ks[16], (DEPTH, MIX_DIM, D_MODEL), MIX_DIM ** -0.5),
        "ffn2_pre_g": gain(ks[17]),
        "ffn2_post_g": gain(ks[18]),
        "ffn2_w_gate": nrm(ks[19], (DEPTH, D_MODEL, D_FF), D_MODEL ** -0.5),
        "ffn2_w_up": nrm(ks[20], (DEPTH, D_MODEL, D_FF), D_MODEL ** -0.5),
        "ffn2_w_down": nrm(ks[21], (DEPTH, D_FF, D_MODEL), D_FF ** -0.5),
    }


def reference(x_prompt, x_sample, state_conv, state_ret, state_pool,
              ffn1_pre_g, ffn1_post_g, ffn1_w_gate, ffn1_w_up, ffn1_w_down,
              mix_pre_g, mix_post_g, w_mix_in, conv_w, pool_w, pool_scale, w_mix_out,
              ffn2_pre_g, ffn2_post_g, ffn2_w_gate, ffn2_w_up, ffn2_w_down):
    p = {
        'ffn1_pre_g': ffn1_pre_g, 'ffn1_post_g': ffn1_post_g,
        'ffn1_w_gate': ffn1_w_gate, 'ffn1_w_up': ffn1_w_up, 'ffn1_w_down': ffn1_w_down,
        'mix_pre_g': mix_pre_g, 'mix_post_g': mix_post_g, 'w_mix_in': w_mix_in,
        'conv_w': conv_w, 'pool_w': pool_w, 'pool_scale': pool_scale, 'w_mix_out': w_mix_out,
        'ffn2_pre_g': ffn2_pre_g, 'ffn2_post_g': ffn2_post_g,
        'ffn2_w_gate': ffn2_w_gate, 'ffn2_w_up': ffn2_w_up, 'ffn2_w_down': ffn2_w_down,
    }
    nb = x_prompt.shape[0]
    dt = x_prompt.dtype
    zc = jnp.zeros((DEPTH, nb, CONV_K - 1, CONV_DIM), dt)
    zr = jnp.zeros((DEPTH, nb, RET_HEADS, RET_HEAD_DIM, RET_HEAD_DIM), dt)
    zp = jnp.zeros((DEPTH, nb, POOL_BUF, POOL_DIM), dt)
    y_prompt, conv_p, ret_p, pool_p = trunk(x_prompt, 0, zc, zr, zp, p)
    y_sample, conv_s, ret_s, pool_s = trunk(x_sample, PAST_LEN, state_conv, state_ret, state_pool, p)
    return (y_prompt, y_sample, conv_p, ret_p, pool_p, conv_s, ret_s, pool_s)
```

```python
import functools

import numpy as np
import jax
import jax.numpy as jnp
from jax import lax
from jax.experimental import pallas as pl
from jax.experimental.pallas import tpu as pltpu

D_MODEL = 1024
DEPTH = 4
PAST_LEN = 16384
CONV_DIM = D_MODEL // 4
RET_DIM = D_MODEL // 2
POOL_DIM = D_MODEL // 4
RET_HEADS = 4
RET_HEAD_DIM = RET_DIM // RET_HEADS
ROPE_BASE = 10000.0
CONV_K = 3
POOL_WINDOWS = (2, 4, 8, 16)
POOL_GROUP_DIM = POOL_DIM // len(POOL_WINDOWS)
POOL_BUF = max(POOL_WINDOWS) - 1
D_FF = 2816
EPS = 1e-6
MIX_IN_DIM = 3 * CONV_DIM + 4 * RET_DIM + POOL_DIM

SUBLANES = 8
LANES = 128
VMEM_LIMIT_BYTES = 56 * 1024 * 1024

CONV_HIST = SUBLANES
POOL_HIST = 2 * SUBLANES

FFN_ROWS = 512
PROMPT_CHUNK = 256
SAMPLE_SEQS = 16

BF16 = jnp.bfloat16
F32 = jnp.float32


def _rms(x, g):
    return x * lax.rsqrt(jnp.mean(x * x, axis=-1, keepdims=True) + EPS) * g


def _dot(a, b):
    return jnp.dot(a, b, preferred_element_type=F32)


def _const_spec(shape):
    zeros = (0,) * len(shape)
    return pl.BlockSpec(shape, lambda *_: zeros, pipeline_mode=pl.Buffered(1))


def _ffn_kernel(x_ref, pre_g_ref, post_g_ref, wg_ref, wu_ref, wd_ref, o_ref):
    x = x_ref[...]
    h = _rms(x, pre_g_ref[...]).astype(BF16)
    gate = _dot(h, wg_ref[...])
    up = _dot(h, wu_ref[...])
    act = (gate * jax.nn.sigmoid(gate) * up).astype(BF16)
    y = _dot(act, wd_ref[...])
    o_ref[...] = x + 0.5 * _rms(y, post_g_ref[...])


def _ffn(x, pre_g, post_g, wg, wu, wd):
    rows = x.shape[0]
    tm = min(FFN_ROWS, rows)
    assert rows % tm == 0
    row_spec = pl.BlockSpec((tm, D_MODEL), lambda i: (i, 0))
    return pl.pallas_call(
        _ffn_kernel,
        grid=(rows // tm,),
        in_specs=[
            row_spec,
            _const_spec((1, D_MODEL)),
            _const_spec((1, D_MODEL)),
            _const_spec((D_MODEL, D_FF)),
            _const_spec((D_MODEL, D_FF)),
            _const_spec((D_FF, D_MODEL)),
        ],
        out_specs=row_spec,
        out_shape=jax.ShapeDtypeStruct((rows, D_MODEL), F32),
        compiler_params=pltpu.CompilerParams(
            dimension_semantics=("arbitrary",), vmem_limit_bytes=VMEM_LIMIT_BYTES),
        name="ffn",
    )(x, pre_g, post_g, wg, wu, wd)


def _conv_from_ext(ext, w_ref):
    w = w_ref[...]
    return (ext * w[2:3, :] + pltpu.roll(ext, 1, 0) * w[1:2, :]
            + pltpu.roll(ext, 2, 0) * w[0:1, :])


def _pool_sums_from_ext(ext):
    w2 = ext + pltpu.roll(ext, 1, 0)
    w4 = w2 + pltpu.roll(w2, 2, 0)
    w8 = w4 + pltpu.roll(w4, 4, 0)
    w16 = w8 + pltpu.roll(w8, 8, 0)
    lane = lax.broadcasted_iota(jnp.int32, ext.shape, 1)
    g = POOL_GROUP_DIM
    return jnp.where(lane < g, w2, jnp.where(lane < 2 * g, w4, jnp.where(lane < 3 * g, w8, w16)))


def _rope(x, cos, sin_signed):
    return x * cos + pltpu.roll(x, RET_HEAD_DIM // 2, 1) * sin_signed


def _group_norm_gate(o, gate):
    mu = jnp.mean(o, axis=-1, keepdims=True)
    c = o - mu
    var = jnp.mean(c * c, axis=-1, keepdims=True)
    return gate * jax.nn.sigmoid(gate) * (c * lax.rsqrt(var + EPS))


def _split_z(z):
    c = CONV_DIM
    r0 = 3 * c
    cx, cb, cc = z[:, 0:c], z[:, c:2 * c], z[:, 2 * c:3 * c]
    q = z[:, r0:r0 + RET_DIM]
    k = z[:, r0 + RET_DIM:r0 + 2 * RET_DIM]
    v = z[:, r0 + 2 * RET_DIM:r0 + 3 * RET_DIM]
    g = z[:, r0 + 3 * RET_DIM:r0 + 4 * RET_DIM]
    pu = z[:, r0 + 4 * RET_DIM:]
    return cx, cb, cc, q, k, v, g, pu


def _head(t, h):
    return t[:, h * RET_HEAD_DIM:(h + 1) * RET_HEAD_DIM]


def _prompt_mixer_kernel(chunk_decay,
                         x_ref, pre_g_ref, post_g_ref, w_in_ref, w_out_ref, conv_w_ref,
                         pool_w_ref, pool_scale_ref, cosq_ref, sinq_ref, cosk_ref, sink_ref,
                         mask_ref, cdec_ref, kdec_ref, cnt_ref,
                         o_ref, conv_o_ref, ret_o_ref, pool_o_ref,
                         conv_ext, pool_ext, s_ref):
    t = pl.program_id(1)
    rows = x_ref.shape[1]

    @pl.when(t == 0)
    def _():
        conv_ext[0:CONV_HIST, :] = jnp.zeros((CONV_HIST, CONV_DIM), F32)
        pool_ext[0:POOL_HIST, :] = jnp.zeros((POOL_HIST, POOL_DIM), F32)
        s_ref[...] = jnp.zeros(s_ref.shape, F32)

    x = x_ref[0]
    h = _rms(x, pre_g_ref[...]).astype(BF16)
    z = _dot(h, w_in_ref[...])
    cx, cb, cc, q, k, v, g, pu = _split_z(z)

    conv_ext[CONV_HIST:, :] = cc * cx
    ext = conv_ext[...]
    yc = cb * _conv_from_ext(ext, conv_w_ref)[CONV_HIST:, :]
    conv_tail = ext[rows:, :]
    conv_ext[0:CONV_HIST, :] = conv_tail
    conv_o_ref[0, 0] = conv_tail

    pool_ext[POOL_HIST:, :] = pu
    ext = pool_ext[...]
    mean = _pool_sums_from_ext(ext)[POOL_HIST:, :] / cnt_ref[...]
    yp = _dot((mean - pu).astype(BF16), pool_w_ref[...]) * pool_scale_ref[...]
    pool_tail = ext[rows:, :]
    pool_ext[0:POOL_HIST, :] = pool_tail
    pool_o_ref[0, 0] = pool_tail

    outs = [yc]
    for hd in range(RET_HEADS):
        qh = _rope(_head(q, hd), cosq_ref[...], sinq_ref[...]).astype(BF16)
        kr = _rope(_head(k, hd), cosk_ref[...], sink_ref[...])
        kh = kr.astype(BF16)
        vh = _head(v, hd).astype(BF16)
        s_old = s_ref[hd]
        scores = lax.dot_general(qh, kh, (((1,), (1,)), ((), ())), preferred_element_type=F32)
        inner = _dot((scores * mask_ref[hd]).astype(BF16), vh)
        cross = _dot(qh, s_old.astype(BF16)) * cdec_ref[hd]
        kd = (kr * kdec_ref[hd]).astype(BF16)
        s_new = s_old * chunk_decay[hd] + lax.dot_general(
            kd, vh, (((0,), (0,)), ((), ())), preferred_element_type=F32)
        s_ref[hd] = s_new
        ret_o_ref[0, 0, hd] = s_new
        outs.append(_group_norm_gate(inner + cross, _head(g, hd)))
    outs.append(yp)

    m = _dot(jnp.concatenate(outs, axis=-1).astype(BF16), w_out_ref[...])
    o_ref[0] = x + _rms(m, post_g_ref[...])


def _sample_mixer_kernel(chunk_decay,
                         x_ref, pre_g_ref, post_g_ref, w_in_ref, w_out_ref, conv_w_ref,
                         pool_w_ref, pool_scale_ref, cosq_ref, sinq_ref, cosk_ref, sink_ref,
                         mask_ref, cdec_ref, kdec_ref, cnt_ref,
                         conv_s_ref, ret_s_ref, pool_s_ref,
                         o_ref, conv_o_ref, ret_o_ref, pool_o_ref,
                         conv_ext, pool_ext):
    nseq, seq = x_ref.shape[0], x_ref.shape[1]
    rows = nseq * seq

    x = x_ref[...].reshape(rows, D_MODEL)
    h = _rms(x, pre_g_ref[...]).astype(BF16)
    z = _dot(h, w_in_ref[...])
    cx, cb, cc, q, k, v, g, pu = _split_z(z)

    conv_ext[:, 0:CONV_HIST, :] = conv_s_ref[0]
    conv_ext[:, CONV_HIST:, :] = (cc * cx).reshape(nseq, seq, CONV_DIM)
    ext = conv_ext[...]
    y = _conv_from_ext(ext.reshape(nseq * (CONV_HIST + seq), CONV_DIM), conv_w_ref)
    y = y.reshape(nseq, CONV_HIST + seq, CONV_DIM)[:, CONV_HIST:, :]
    yc = cb * y.reshape(rows, CONV_DIM)
    conv_o_ref[0] = ext[:, seq:, :]

    pool_ext[:, 0:POOL_HIST, :] = pool_s_ref[0]
    pool_ext[:, POOL_HIST:, :] = pu.reshape(nseq, seq, POOL_DIM)
    ext = pool_ext[...]
    sums = _pool_sums_from_ext(ext.reshape(nseq * (POOL_HIST + seq), POOL_DIM))
    sums = sums.reshape(nseq, POOL_HIST + seq, POOL_DIM)[:, POOL_HIST:, :]
    mean = sums.reshape(rows, POOL_DIM) / cnt_ref[...]
    yp = _dot((mean - pu).astype(BF16), pool_w_ref[...]) * pool_scale_ref[...]
    pool_o_ref[0] = ext[:, seq:, :]

    outs = [yc]
    for hd in range(RET_HEADS):
        qh = _rope(_head(q, hd), cosq_ref[...], sinq_ref[...]).astype(BF16)
        kr = _rope(_head(k, hd), cosk_ref[...], sink_ref[...])
        kh = kr.astype(BF16)
        vh = _head(v, hd).astype(BF16)
        s_old = ret_s_ref[0, :, hd]
        scores = lax.dot_general(qh, kh, (((1,), (1,)), ((), ())), preferred_element_type=F32)
        inner = _dot((scores * mask_ref[hd]).astype(BF16), vh)
        q3 = qh.reshape(nseq, seq, RET_HEAD_DIM)
        cross = lax.dot_general(q3, s_old.astype(BF16), (((2,), (1,)), ((0,), (0,))),
                                preferred_element_type=F32)
        cross = cross.reshape(rows, RET_HEAD_DIM) * cdec_ref[hd]
        kd3 = (kr * kdec_ref[hd]).astype(BF16).reshape(nseq, seq, RET_HEAD_DIM)
        v3 = vh.reshape(nseq, seq, RET_HEAD_DIM)
        kv = lax.dot_general(kd3, v3, (((1,), (1,)), ((0,), (0,))), preferred_element_type=F32)
        ret_o_ref[0, :, hd] = s_old * chunk_decay[hd] + kv
        outs.append(_group_norm_gate(inner + cross, _head(g, hd)))
    outs.append(yp)

    m = _dot(jnp.concatenate(outs, axis=-1).astype(BF16), w_out_ref[...])
    o_ref[...] = (x + _rms(m, post_g_ref[...])).reshape(nseq, seq, D_MODEL)


def _log_gamma():
    return np.log(1.0 - 2.0 ** (-5.0 - np.arange(RET_HEADS, dtype=np.float64)))


def _rope_tables(pos):
    half = RET_HEAD_DIM // 2
    inv = ROPE_BASE ** (-np.arange(half, dtype=np.float64) / half)
    ang = pos.astype(np.float64)[:, None] * inv[None, :]
    cos, sin = np.cos(ang), np.sin(ang)
    return np.concatenate([cos, cos], -1), np.concatenate([-sin, sin], -1)


def _retention_tables(seq_of_row, idx_of_row, chunk):
    lg = _log_gamma()[:, None, None]
    diff = (idx_of_row[:, None] - idx_of_row[None, :]).astype(np.float64)
    same = seq_of_row[:, None] == seq_of_row[None, :]
    mask = np.where(same & (diff >= 0), np.exp(np.maximum(diff, 0.0)[None] * lg), 0.0)
    ones = np.ones((1, 1, RET_HEAD_DIM))
    cdec = np.exp((idx_of_row + 1.0)[None, :, None] * lg) * ones
    kdec = np.exp((chunk - 1.0 - idx_of_row)[None, :, None] * lg) * ones
    chunk_decay = tuple(float(d) for d in np.exp(chunk * _log_gamma()))
    return mask, cdec, kdec, chunk_decay


def _pool_counts(pos):
    win = np.repeat(np.asarray(POOL_WINDOWS, np.float64), POOL_GROUP_DIM)
    return np.minimum(pos.astype(np.float64)[:, None] + 1.0, win[None, :])


def _f32(a):
    return jnp.asarray(np.asarray(a, np.float32))


def _mixer_weight_specs():
    return [
        _const_spec((1, D_MODEL)),
        _const_spec((1, D_MODEL)),
        _const_spec((D_MODEL, MIX_IN_DIM)),
        _const_spec((D_MODEL, D_MODEL)),
        _const_spec((SUBLANES, CONV_DIM)),
        _const_spec((POOL_DIM, POOL_DIM)),
        _const_spec((1, POOL_DIM)),
    ]


def _prompt_mixer(x, lw):
    batch, seq, _ = x.shape
    c = PROMPT_CHUNK
    assert seq % c == 0
    pos = np.arange(seq)
    cos, sin = _rope_tables(pos)
    scale = RET_HEAD_DIM ** -0.5
    idx = np.arange(c)
    mask, cdec, kdec, chunk_decay = _retention_tables(np.zeros(c, np.int64), idx, c)
    cnt = _pool_counts(pos)

    def pos_spec(width):
        return pl.BlockSpec((c, width), lambda b, t: (t, 0))

    x_spec = pl.BlockSpec((1, c, D_MODEL), lambda b, t: (b, t, 0))
    y, conv_o, ret_o, pool_o = pl.pallas_call(
        functools.partial(_prompt_mixer_kernel, chunk_decay),
        grid=(batch, seq // c),
        in_specs=[x_spec] + _mixer_weight_specs() + [
            pos_spec(RET_HEAD_DIM), pos_spec(RET_HEAD_DIM), pos_spec(RET_HEAD_DIM),
            pos_spec(RET_HEAD_DIM),
            _const_spec((RET_HEADS, c, c)),
            _const_spec((RET_HEADS, c, RET_HEAD_DIM)),
            _const_spec((RET_HEADS, c, RET_HEAD_DIM)),
            pos_spec(POOL_DIM),
        ],
        out_specs=[
            x_spec,
            pl.BlockSpec((1, 1, CONV_HIST, CONV_DIM), lambda b, t: (b, 0, 0, 0)),
            pl.BlockSpec((1, 1, RET_HEADS, RET_HEAD_DIM, RET_HEAD_DIM), lambda b, t: (b, 0, 0, 0, 0)),
            pl.BlockSpec((1, 1, POOL_HIST, POOL_DIM), lambda b, t: (b, 0, 0, 0)),
        ],
        out_shape=[
            jax.ShapeDtypeStruct(x.shape, F32),
            jax.ShapeDtypeStruct((batch, 1, CONV_HIST, CONV_DIM), F32),
            jax.ShapeDtypeStruct((batch, 1, RET_HEADS, RET_HEAD_DIM, RET_HEAD_DIM), F32),
            jax.ShapeDtypeStruct((batch, 1, POOL_HIST, POOL_DIM), F32),
        ],
        scratch_shapes=[
            pltpu.VMEM((CONV_HIST + c, CONV_DIM), F32),
            pltpu.VMEM((POOL_HIST + c, POOL_DIM), F32),
            pltpu.VMEM((RET_HEADS, RET_HEAD_DIM, RET_HEAD_DIM), F32),
        ],
        compiler_params=pltpu.CompilerParams(
            dimension_semantics=("arbitrary", "arbitrary"), vmem_limit_bytes=VMEM_LIMIT_BYTES),
        name="prompt_mixer",
    )(x, *lw, _f32(cos * scale), _f32(sin * scale), _f32(cos), _f32(sin),
      _f32(mask), _f32(cdec), _f32(kdec), _f32(cnt))
    return (y, conv_o[:, 0, CONV_HIST - (CONV_K - 1):], ret_o[:, 0],
            pool_o[:, 0, POOL_HIST - POOL_BUF:])


def _sample_mixer(x, lw, conv_s, ret_s, pool_s, pos0):
    batch, seq, _ = x.shape
    n = SAMPLE_SEQS
    assert seq == SUBLANES and batch % n == 0
    rows = n * seq
    seq_of_row = np.repeat(np.arange(n), seq)
    idx_of_row = np.tile(np.arange(seq), n)
    cos, sin = _rope_tables(pos0 + idx_of_row)
    scale = RET_HEAD_DIM ** -0.5
    mask, cdec, kdec, chunk_decay = _retention_tables(seq_of_row, idx_of_row, seq)
    cnt = _pool_counts(pos0 + idx_of_row)

    x_spec = pl.BlockSpec((n, seq, D_MODEL), lambda b: (b, 0, 0))
    conv_spec = pl.BlockSpec((1, n, CONV_HIST, CONV_DIM), lambda b: (0, b, 0, 0))
    ret_spec = pl.BlockSpec((1, n, RET_HEADS, RET_HEAD_DIM, RET_HEAD_DIM), lambda b: (0, b, 0, 0, 0))
    pool_spec = pl.BlockSpec((1, n, POOL_HIST, POOL_DIM), lambda b: (0, b, 0, 0))
    y, conv_o, ret_o, pool_o = pl.pallas_call(
        functools.partial(_sample_mixer_kernel, chunk_decay),
        grid=(batch // n,),
        in_specs=[x_spec] + _mixer_weight_specs() + [
            _const_spec((rows, RET_HEAD_DIM)), _const_spec((rows, RET_HEAD_DIM)),
            _const_spec((rows, RET_HEAD_DIM)), _const_spec((rows, RET_HEAD_DIM)),
            _const_spec((RET_HEADS, rows, rows)),
            _const_spec((RET_HEADS, rows, RET_HEAD_DIM)),
            _const_spec((RET_HEADS, rows, RET_HEAD_DIM)),
            _const_spec((rows, POOL_DIM)),
            conv_spec, ret_spec, pool_spec,
        ],
        out_specs=[x_spec, conv_spec, ret_spec, pool_spec],
        out_shape=[
            jax.ShapeDtypeStruct(x.shape, F32),
            jax.ShapeDtypeStruct((1, batch, CONV_HIST, CONV_DIM), F32),
            jax.ShapeDtypeStruct((1, batch, RET_HEADS, RET_HEAD_DIM, RET_HEAD_DIM), F32),
            jax.ShapeDtypeStruct((1, batch, POOL_HIST, POOL_DIM), F32),
        ],
        scratch_shapes=[
            pltpu.VMEM((n, CONV_HIST + seq, CONV_DIM), F32),
            pltpu.VMEM((n, POOL_HIST + seq, POOL_DIM), F32),
        ],
        compiler_params=pltpu.CompilerParams(
            dimension_semantics=("arbitrary",), vmem_limit_bytes=VMEM_LIMIT_BYTES),
        name="sample_mixer",
    )(x, *lw, _f32(cos * scale), _f32(sin * scale), _f32(cos), _f32(sin),
      _f32(mask), _f32(cdec), _f32(kdec), _f32(cnt), conv_s, ret_s, pool_s)
    return (y, conv_o[0, :, CONV_HIST - (CONV_K - 1):], ret_o[0],
            pool_o[0, :, POOL_HIST - POOL_BUF:])


def _block_diag_pool_weight(w):
    groups, c, _ = w.shape
    eye = jnp.eye(groups, dtype=w.dtype)
    return (eye[:, None, :, None] * w[:, :, None, :]).reshape(groups * c, groups * c)


def kernel(x_prompt, x_sample, state_conv, state_ret, state_pool, ffn1_pre_g, ffn1_post_g, ffn1_w_gate, ffn1_w_up, ffn1_w_down, mix_pre_g, mix_post_g, w_mix_in, conv_w, pool_w, pool_scale, w_mix_out, ffn2_pre_g, ffn2_post_g, ffn2_w_gate, ffn2_w_up, ffn2_w_down):
    nb, seq, _ = x_prompt.shape
    db, dseq, _ = x_sample.shape

    conv_s = jnp.pad(state_conv, ((0, 0), (0, 0), (CONV_HIST - (CONV_K - 1), 0), (0, 0)))
    pool_s = jnp.pad(state_pool, ((0, 0), (0, 0), (POOL_HIST - POOL_BUF, 0), (0, 0)))
    conv_taps = jnp.pad(conv_w, ((0, 0), (0, SUBLANES - CONV_K), (0, 0)))

    xp, xs = x_prompt, x_sample
    prompt_states, sample_states = [], []
    for l in range(DEPTH):
        row = lambda a: a[l][None, :]
        ffn1 = (row(ffn1_pre_g), row(ffn1_post_g), ffn1_w_gate[l].astype(BF16),
                ffn1_w_up[l].astype(BF16), ffn1_w_down[l].astype(BF16))
        ffn2 = (row(ffn2_pre_g), row(ffn2_post_g), ffn2_w_gate[l].astype(BF16),
                ffn2_w_up[l].astype(BF16), ffn2_w_down[l].astype(BF16))
        lw = (row(mix_pre_g), row(mix_post_g), w_mix_in[l].astype(BF16),
              w_mix_out[l].astype(BF16), conv_taps[l],
              _block_diag_pool_weight(pool_w[l]).astype(BF16), row(pool_scale))

        xp = _ffn(xp.reshape(nb * seq, D_MODEL), *ffn1).reshape(nb, seq, D_MODEL)
        xs = _ffn(xs.reshape(db * dseq, D_MODEL), *ffn1).reshape(db, dseq, D_MODEL)

        xp, *pst = _prompt_mixer(xp, lw)
        xs, *sst = _sample_mixer(xs, lw, conv_s[l:l + 1], state_ret[l:l + 1], pool_s[l:l + 1],
                                 PAST_LEN)
        prompt_states.append(pst)
        sample_states.append(sst)

        xp = _ffn(xp.reshape(nb * seq, D_MODEL), *ffn2).reshape(nb, seq, D_MODEL)
        xs = _ffn(xs.reshape(db * dseq, D_MODEL), *ffn2).reshape(db, dseq, D_MODEL)

    stack = lambda states, i: jnp.stack([s[i] for s in states])
    return (xp, xs,
            stack(prompt_states, 0), stack(prompt_states, 1), stack(prompt_states, 2),
            stack(sample_states, 0), stack(sample_states, 1), stack(sample_states, 2))
```

```python
import functools

import numpy as np
import jax
import jax.numpy as jnp
from jax import lax
from jax.experimental import pallas as pl
from jax.experimental.pallas import tpu as pltpu

D_MODEL = 1024
DEPTH = 4
PAST_LEN = 16384
CONV_DIM = D_MODEL // 4
RET_DIM = D_MODEL // 2
POOL_DIM = D_MODEL // 4
RET_HEADS = 4
RET_HEAD_DIM = RET_DIM // RET_HEADS
ROPE_BASE = 10000.0
CONV_K = 3
POOL_WINDOWS = (2, 4, 8, 16)
POOL_GROUP_DIM = POOL_DIM // len(POOL_WINDOWS)
POOL_BUF = max(POOL_WINDOWS) - 1
D_FF = 2816
EPS = 1e-6
MIX_IN_DIM = 3 * CONV_DIM + 4 * RET_DIM + POOL_DIM

SUBLANES = 8
VMEM_LIMIT_BYTES = 56 * 1024 * 1024

CONV_HIST = SUBLANES
POOL_HIST = 2 * SUBLANES

FFN_ROWS = 512
PROMPT_ROWS = 512
RET_CHUNK = 256
SAMPLE_SEQS = 16

BF16 = jnp.bfloat16
F32 = jnp.float32


def _rms(x, g):
    return x * lax.rsqrt(jnp.mean(x * x, axis=-1, keepdims=True) + EPS) * g


def _dot(a, b):
    return jnp.dot(a, b, preferred_element_type=F32)


def _const_spec(shape):
    zeros = (0,) * len(shape)
    return pl.BlockSpec(shape, lambda *_: zeros, pipeline_mode=pl.Buffered(1))


def _layer_spec(layer, shape):
    zeros = (0,) * len(shape)
    return pl.BlockSpec((None,) + shape, lambda *_: (layer,) + zeros, pipeline_mode=pl.Buffered(1))


def _ffn_body(x, pre_g_ref, post_g_ref, wg_ref, wu_ref, wd_ref):
    h = _rms(x, pre_g_ref[...]).astype(BF16)
    gate = _dot(h, wg_ref[...])
    up = _dot(h, wu_ref[...])
    act = (gate * jax.nn.sigmoid(gate) * up).astype(BF16)
    return x + 0.5 * _rms(_dot(act, wd_ref[...]), post_g_ref[...])


def _ffn_kernel(x_ref, pre_g_ref, post_g_ref, wg_ref, wu_ref, wd_ref, o_ref):
    o_ref[...] = _ffn_body(x_ref[...], pre_g_ref, post_g_ref, wg_ref, wu_ref, wd_ref)


def _ffn_join_kernel(split, xa_ref, xb_ref, pre_g_ref, post_g_ref, wg_ref, wu_ref, wd_ref, o_ref):
    i = pl.program_id(0)

    @pl.when(i < split)
    def _():
        o_ref[...] = _ffn_body(xa_ref[...], pre_g_ref, post_g_ref, wg_ref, wu_ref, wd_ref)

    @pl.when(i >= split)
    def _():
        o_ref[...] = _ffn_body(xb_ref[...], pre_g_ref, post_g_ref, wg_ref, wu_ref, wd_ref)


def _ffn_fork_kernel(split, x_ref, pre_g_ref, post_g_ref, wg_ref, wu_ref, wd_ref, oa_ref, ob_ref):
    i = pl.program_id(0)

    @pl.when(i < split)
    def _():
        oa_ref[...] = _ffn_body(x_ref[...], pre_g_ref, post_g_ref, wg_ref, wu_ref, wd_ref)

    @pl.when(i >= split)
    def _():
        ob_ref[...] = _ffn_body(x_ref[...], pre_g_ref, post_g_ref, wg_ref, wu_ref, wd_ref)


def _ffn(xs, layer, weights, rows_a, rows_b, mode):
    tm = FFN_ROWS
    assert rows_a % tm == 0 and rows_b % tm == 0
    split = rows_a // tm
    steps = (rows_a + rows_b) // tm
    row_spec = pl.BlockSpec((tm, D_MODEL), lambda i: (i, 0))
    a_spec = pl.BlockSpec((tm, D_MODEL), lambda i: (jnp.minimum(i, split - 1), 0))
    b_spec = pl.BlockSpec((tm, D_MODEL), lambda i: (jnp.maximum(i - split, 0), 0))
    stream_shape = jax.ShapeDtypeStruct((rows_a + rows_b, D_MODEL), F32)
    weight_specs = [
        _layer_spec(layer, (1, D_MODEL)),
        _layer_spec(layer, (1, D_MODEL)),
        _layer_spec(layer, (D_MODEL, D_FF)),
        _layer_spec(layer, (D_MODEL, D_FF)),
        _layer_spec(layer, (D_FF, D_MODEL)),
    ]
    if mode == "join":
        body, x_specs = functools.partial(_ffn_join_kernel, split), [a_spec, b_spec]
        out_specs, out_shape = row_spec, stream_shape
    elif mode == "fork":
        body, x_specs = functools.partial(_ffn_fork_kernel, split), [row_spec]
        out_specs = [a_spec, b_spec]
        out_shape = [jax.ShapeDtypeStruct((rows_a, D_MODEL), F32),
                     jax.ShapeDtypeStruct((rows_b, D_MODEL), F32)]
    else:
        body, x_specs, out_specs, out_shape = _ffn_kernel, [row_spec], row_spec, stream_shape
    return pl.pallas_call(
        body,
        grid=(steps,),
        in_specs=x_specs + weight_specs,
        out_specs=out_specs,
        out_shape=out_shape,
        compiler_params=pltpu.CompilerParams(
            dimension_semantics=("arbitrary",), vmem_limit_bytes=VMEM_LIMIT_BYTES),
        name="ffn_" + mode,
    )(*xs, *weights)


def _conv_from_ext(ext, w_ref):
    w = w_ref[...]
    return (ext * w[2:3, :] + pltpu.roll(ext, 1, 0) * w[1:2, :]
            + pltpu.roll(ext, 2, 0) * w[0:1, :])


def _pool_sums_from_ext(ext):
    w2 = ext + pltpu.roll(ext, 1, 0)
    w4 = w2 + pltpu.roll(w2, 2, 0)
    w8 = w4 + pltpu.roll(w4, 4, 0)
    w16 = w8 + pltpu.roll(w8, 8, 0)
    lane = lax.broadcasted_iota(jnp.int32, ext.shape, 1)
    g = POOL_GROUP_DIM
    return jnp.where(lane < g, w2, jnp.where(lane < 2 * g, w4, jnp.where(lane < 3 * g, w8, w16)))


def _rope(x, cos, sin_signed):
    return x * cos + pltpu.roll(x, RET_HEAD_DIM // 2, 1) * sin_signed


def _group_norm_gate(o, gate):
    mu = jnp.mean(o, axis=-1, keepdims=True)
    c = o - mu
    var = jnp.mean(c * c, axis=-1, keepdims=True)
    return gate * jax.nn.sigmoid(gate) * (c * lax.rsqrt(var + EPS))


def _split_z(z):
    c = CONV_DIM
    r0 = 3 * c
    cx, cb, cc = z[:, 0:c], z[:, c:2 * c], z[:, 2 * c:3 * c]
    q = z[:, r0:r0 + RET_DIM]
    k = z[:, r0 + RET_DIM:r0 + 2 * RET_DIM]
    v = z[:, r0 + 2 * RET_DIM:r0 + 3 * RET_DIM]
    g = z[:, r0 + 3 * RET_DIM:r0 + 4 * RET_DIM]
    pu = z[:, r0 + 4 * RET_DIM:]
    return cx, cb, cc, q, k, v, g, pu


def _head(t, h):
    return t[:, h * RET_HEAD_DIM:(h + 1) * RET_HEAD_DIM]


def _prompt_mixer_kernel(chunk_decay,
                         x_ref, pre_g_ref, post_g_ref, w_in_ref, w_out_ref, conv_w_ref,
                         pool_w_ref, pool_scale_ref, cosq_ref, sinq_ref, cosk_ref, sink_ref,
                         mask_ref, cdec_ref, kdec_ref, cnt_ref,
                         o_ref, conv_o_ref, ret_o_ref, pool_o_ref,
                         conv_ext, pool_ext, s_ref):
    t = pl.program_id(1)
    rows = x_ref.shape[0]
    chunk = mask_ref.shape[1]

    @pl.when(t == 0)
    def _():
        conv_ext[0:CONV_HIST, :] = jnp.zeros((CONV_HIST, CONV_DIM), F32)
        pool_ext[0:POOL_HIST, :] = jnp.zeros((POOL_HIST, POOL_DIM), F32)
        s_ref[...] = jnp.zeros(s_ref.shape, F32)

    x = x_ref[...]
    h = _rms(x, pre_g_ref[...]).astype(BF16)
    z = _dot(h, w_in_ref[...])
    cx, cb, cc, q, k, v, g, pu = _split_z(z)

    conv_ext[CONV_HIST:, :] = cc * cx
    ext = conv_ext[...]
    yc = cb * _conv_from_ext(ext, conv_w_ref)[CONV_HIST:, :]
    conv_tail = ext[rows:, :]
    conv_ext[0:CONV_HIST, :] = conv_tail
    conv_o_ref[0] = conv_tail

    pool_ext[POOL_HIST:, :] = pu
    ext = pool_ext[...]
    mean = _pool_sums_from_ext(ext)[POOL_HIST:, :] / cnt_ref[...]
    yp = _dot((mean - pu).astype(BF16), pool_w_ref[...]) * pool_scale_ref[...]
    pool_tail = ext[rows:, :]
    pool_ext[0:POOL_HIST, :] = pool_tail
    pool_o_ref[0] = pool_tail

    outs = [yc]
    for hd in range(RET_HEADS):
        s = s_ref[hd]
        head_out = []
        for c in range(rows // chunk):
            sl = slice(c * chunk, (c + 1) * chunk)
            qh = _rope(_head(q, hd)[sl], cosq_ref[sl, :], sinq_ref[sl, :]).astype(BF16)
            kr = _rope(_head(k, hd)[sl], cosk_ref[sl, :], sink_ref[sl, :])
            kh = kr.astype(BF16)
            vh = _head(v, hd)[sl].astype(BF16)
            scores = lax.dot_general(qh, kh, (((1,), (1,)), ((), ())), preferred_element_type=F32)
            inner = _dot((scores * mask_ref[hd]).astype(BF16), vh)
            cross = _dot(qh, s.astype(BF16)) * cdec_ref[hd]
            kd = (kr * kdec_ref[hd]).astype(BF16)
            s = s * chunk_decay[hd] + lax.dot_general(
                kd, vh, (((0,), (0,)), ((), ())), preferred_element_type=F32)
            head_out.append(inner + cross)
        s_ref[hd] = s
        ret_o_ref[0, hd] = s
        outs.append(_group_norm_gate(jnp.concatenate(head_out, axis=0), _head(g, hd)))
    outs.append(yp)

    m = _dot(jnp.concatenate(outs, axis=-1).astype(BF16), w_out_ref[...])
    o_ref[...] = x + _rms(m, post_g_ref[...])


def _sample_mixer_kernel(chunk_decay, aliased_ret,
                         x_ref, pre_g_ref, post_g_ref, w_in_ref, w_out_ref, conv_w_ref,
                         pool_w_ref, pool_scale_ref, cosq_ref, sinq_ref, cosk_ref, sink_ref,
                         mask_ref, cdec_ref, kdec_ref, cnt_ref,
                         conv_s_ref, ret_s_ref, pool_s_ref, *rest):
    o_ref, conv_o_ref, ret_o_ref, pool_o_ref, conv_ext, pool_ext = rest[1 if aliased_ret else 0:]
    nseq, seq = conv_s_ref.shape[0], x_ref.shape[0] // conv_s_ref.shape[0]
    rows = nseq * seq

    x = x_ref[...]
    h = _rms(x, pre_g_ref[...]).astype(BF16)
    z = _dot(h, w_in_ref[...])
    cx, cb, cc, q, k, v, g, pu = _split_z(z)

    conv_ext[:, 0:CONV_HIST, :] = conv_s_ref[...]
    conv_ext[:, CONV_HIST:, :] = (cc * cx).reshape(nseq, seq, CONV_DIM)
    ext = conv_ext[...]
    y = _conv_from_ext(ext.reshape(nseq * (CONV_HIST + seq), CONV_DIM), conv_w_ref)
    y = y.reshape(nseq, CONV_HIST + seq, CONV_DIM)[:, CONV_HIST:, :]
    yc = cb * y.reshape(rows, CONV_DIM)
    conv_o_ref[...] = ext[:, seq:, :]

    pool_ext[:, 0:POOL_HIST, :] = pool_s_ref[...]
    pool_ext[:, POOL_HIST:, :] = pu.reshape(nseq, seq, POOL_DIM)
    ext = pool_ext[...]
    sums = _pool_sums_from_ext(ext.reshape(nseq * (POOL_HIST + seq), POOL_DIM))
    sums = sums.reshape(nseq, POOL_HIST + seq, POOL_DIM)[:, POOL_HIST:, :]
    mean = sums.reshape(rows, POOL_DIM) / cnt_ref[...]
    yp = _dot((mean - pu).astype(BF16), pool_w_ref[...]) * pool_scale_ref[...]
    pool_o_ref[...] = ext[:, seq:, :]

    outs = [yc]
    for hd in range(RET_HEADS):
        qh = _rope(_head(q, hd), cosq_ref[...], sinq_ref[...]).astype(BF16)
        kr = _rope(_head(k, hd), cosk_ref[...], sink_ref[...])
        kh = kr.astype(BF16)
        vh = _head(v, hd).astype(BF16)
        s_old = ret_s_ref[:, hd]
        scores = lax.dot_general(qh, kh, (((1,), (1,)), ((), ())), preferred_element_type=F32)
        inner = _dot((scores * mask_ref[hd]).astype(BF16), vh)
        q3 = qh.reshape(nseq, seq, RET_HEAD_DIM)
        cross = lax.dot_general(q3, s_old.astype(BF16), (((2,), (1,)), ((0,), (0,))),
                                preferred_element_type=F32)
        cross = cross.reshape(rows, RET_HEAD_DIM) * cdec_ref[hd]
        kd3 = (kr * kdec_ref[hd]).astype(BF16).reshape(nseq, seq, RET_HEAD_DIM)
        v3 = vh.reshape(nseq, seq, RET_HEAD_DIM)
        kv = lax.dot_general(kd3, v3, (((1,), (1,)), ((0,), (0,))), preferred_element_type=F32)
        ret_o_ref[:, hd] = s_old * chunk_decay[hd] + kv
        outs.append(_group_norm_gate(inner + cross, _head(g, hd)))
    outs.append(yp)

    m = _dot(jnp.concatenate(outs, axis=-1).astype(BF16), w_out_ref[...])
    o_ref[...] = x + _rms(m, post_g_ref[...])


def _log_gamma():
    return np.log(1.0 - 2.0 ** (-5.0 - np.arange(RET_HEADS, dtype=np.float64)))


def _rope_tables(pos):
    half = RET_HEAD_DIM // 2
    inv = ROPE_BASE ** (-np.arange(half, dtype=np.float64) / half)
    ang = pos.astype(np.float64)[:, None] * inv[None, :]
    cos, sin = np.cos(ang), np.sin(ang)
    return np.concatenate([cos, cos], -1), np.concatenate([-sin, sin], -1)


def _retention_tables(seq_of_row, idx_of_row, chunk):
    lg = _log_gamma()[:, None, None]
    diff = (idx_of_row[:, None] - idx_of_row[None, :]).astype(np.float64)
    same = seq_of_row[:, None] == seq_of_row[None, :]
    mask = np.where(same & (diff >= 0), np.exp(np.maximum(diff, 0.0)[None] * lg), 0.0)
    ones = np.ones((1, 1, RET_HEAD_DIM))
    cdec = np.exp((idx_of_row + 1.0)[None, :, None] * lg) * ones
    kdec = np.exp((chunk - 1.0 - idx_of_row)[None, :, None] * lg) * ones
    chunk_decay = tuple(float(d) for d in np.exp(chunk * _log_gamma()))
    return mask, cdec, kdec, chunk_decay


def _pool_counts(pos):
    win = np.repeat(np.asarray(POOL_WINDOWS, np.float64), POOL_GROUP_DIM)
    return np.minimum(pos.astype(np.float64)[:, None] + 1.0, win[None, :])


def _f32(a):
    return jnp.asarray(np.asarray(a, np.float32))


def _mixer_weight_specs(layer):
    return [
        _layer_spec(layer, (1, D_MODEL)),
        _layer_spec(layer, (1, D_MODEL)),
        _layer_spec(layer, (D_MODEL, MIX_IN_DIM)),
        _layer_spec(layer, (D_MODEL, D_MODEL)),
        _layer_spec(layer, (SUBLANES, CONV_DIM)),
        _layer_spec(layer, (POOL_DIM, POOL_DIM)),
        _layer_spec(layer, (1, POOL_DIM)),
    ]


def _prompt_mixer(stream, layer, weights, batch, seq):
    rows, c = PROMPT_ROWS, RET_CHUNK
    assert seq % rows == 0 and rows % c == 0
    steps = seq // rows
    pos = np.arange(seq)
    cos, sin = _rope_tables(pos)
    scale = RET_HEAD_DIM ** -0.5
    mask, cdec, kdec, chunk_decay = _retention_tables(np.zeros(c, np.int64), np.arange(c), c)
    cnt = _pool_counts(pos)

    def pos_spec(width):
        return pl.BlockSpec((rows, width), lambda b, t: (t, 0))

    def state_spec(*shape):
        zeros = (0,) * len(shape)
        return pl.BlockSpec((1,) + shape, lambda b, t: (b,) + zeros)

    x_spec = pl.BlockSpec((rows, D_MODEL), lambda b, t: (b * steps + t, 0))
    return pl.pallas_call(
        functools.partial(_prompt_mixer_kernel, chunk_decay),
        grid=(batch, steps),
        in_specs=[x_spec] + _mixer_weight_specs(layer) + [
            pos_spec(RET_HEAD_DIM), pos_spec(RET_HEAD_DIM), pos_spec(RET_HEAD_DIM),
            pos_spec(RET_HEAD_DIM),
            _const_spec((RET_HEADS, c, c)),
            _const_spec((RET_HEADS, c, RET_HEAD_DIM)),
            _const_spec((RET_HEADS, c, RET_HEAD_DIM)),
            pos_spec(POOL_DIM),
        ],
        out_specs=[
            x_spec,
            state_spec(CONV_HIST, CONV_DIM),
            state_spec(RET_HEADS, RET_HEAD_DIM, RET_HEAD_DIM),
            state_spec(POOL_HIST, POOL_DIM),
        ],
        out_shape=[
            jax.ShapeDtypeStruct(stream.shape, F32),
            jax.ShapeDtypeStruct((batch, CONV_HIST, CONV_DIM), F32),
            jax.ShapeDtypeStruct((batch, RET_HEADS, RET_HEAD_DIM, RET_HEAD_DIM), F32),
            jax.ShapeDtypeStruct((batch, POOL_HIST, POOL_DIM), F32),
        ],
        scratch_shapes=[
            pltpu.VMEM((CONV_HIST + rows, CONV_DIM), F32),
            pltpu.VMEM((POOL_HIST + rows, POOL_DIM), F32),
            pltpu.VMEM((RET_HEADS, RET_HEAD_DIM, RET_HEAD_DIM), F32),
        ],
        input_output_aliases={0: 0},
        compiler_params=pltpu.CompilerParams(
            dimension_semantics=("arbitrary", "arbitrary"), vmem_limit_bytes=VMEM_LIMIT_BYTES),
        name="prompt_mixer",
    )(stream, *weights, _f32(cos * scale), _f32(sin * scale), _f32(cos), _f32(sin),
      _f32(mask), _f32(cdec), _f32(kdec), _f32(cnt))


def _sample_mixer(stream, layer, weights, conv_s, ret_s, pool_s, ret_acc, row0, batch, seq, pos0):
    n = SAMPLE_SEQS
    rows = n * seq
    assert seq == SUBLANES and batch % n == 0 and row0 % rows == 0
    seq_of_row = np.repeat(np.arange(n), seq)
    idx_of_row = np.tile(np.arange(seq), n)
    cos, sin = _rope_tables(pos0 + idx_of_row)
    scale = RET_HEAD_DIM ** -0.5
    mask, cdec, kdec, chunk_decay = _retention_tables(seq_of_row, idx_of_row, seq)
    cnt = _pool_counts(pos0 + idx_of_row)

    def state_spec(*shape):
        zeros = (0,) * len(shape)
        return pl.BlockSpec((None, n) + shape, lambda b: (layer, b) + zeros)

    x_spec = pl.BlockSpec((rows, D_MODEL), lambda b: (row0 // rows + b, 0))
    conv_spec = state_spec(CONV_HIST, CONV_DIM)
    ret_spec = state_spec(RET_HEADS, RET_HEAD_DIM, RET_HEAD_DIM)
    pool_spec = state_spec(POOL_HIST, POOL_DIM)
    in_specs = [x_spec] + _mixer_weight_specs(layer) + [
        _const_spec((rows, RET_HEAD_DIM)), _const_spec((rows, RET_HEAD_DIM)),
        _const_spec((rows, RET_HEAD_DIM)), _const_spec((rows, RET_HEAD_DIM)),
        _const_spec((RET_HEADS, rows, rows)),
        _const_spec((RET_HEADS, rows, RET_HEAD_DIM)),
        _const_spec((RET_HEADS, rows, RET_HEAD_DIM)),
        _const_spec((rows, POOL_DIM)),
        conv_spec, ret_spec, pool_spec,
    ]
    args = [stream, *weights, _f32(cos * scale), _f32(sin * scale), _f32(cos), _f32(sin),
            _f32(mask), _f32(cdec), _f32(kdec), _f32(cnt), conv_s, ret_s, pool_s]
    aliases = {0: 0}
    if ret_acc is not None:
        in_specs.append(pl.BlockSpec(memory_space=pl.ANY))
        aliases[len(args)] = 2
        args.append(ret_acc)
    return pl.pallas_call(
        functools.partial(_sample_mixer_kernel, chunk_decay, ret_acc is not None),
        grid=(batch // n,),
        in_specs=in_specs,
        out_specs=[
            x_spec,
            pl.BlockSpec((n, CONV_HIST, CONV_DIM), lambda b: (b, 0, 0)),
            ret_spec,
            pl.BlockSpec((n, POOL_HIST, POOL_DIM), lambda b: (b, 0, 0)),
        ],
        out_shape=[
            jax.ShapeDtypeStruct(stream.shape, F32),
            jax.ShapeDtypeStruct((batch, CONV_HIST, CONV_DIM), F32),
            jax.ShapeDtypeStruct(ret_s.shape, F32),
            jax.ShapeDtypeStruct((batch, POOL_HIST, POOL_DIM), F32),
        ],
        scratch_shapes=[
            pltpu.VMEM((n, CONV_HIST + seq, CONV_DIM), F32),
            pltpu.VMEM((n, POOL_HIST + seq, POOL_DIM), F32),
        ],
        input_output_aliases=aliases,
        compiler_params=pltpu.CompilerParams(
            dimension_semantics=("arbitrary",), vmem_limit_bytes=VMEM_LIMIT_BYTES),
        name="sample_mixer",
    )(*args)


def _block_diag_pool_weight(w):
    depth, groups, c, _ = w.shape
    eye = jnp.eye(groups, dtype=w.dtype)
    return (eye[None, :, None, :, None] * w[:, :, :, None, :]).reshape(depth, groups * c, groups * c)


def kernel(x_prompt, x_sample, state_conv, state_ret, state_pool, ffn1_pre_g, ffn1_post_g, ffn1_w_gate, ffn1_w_up, ffn1_w_down, mix_pre_g, mix_post_g, w_mix_in, conv_w, pool_w, pool_scale, w_mix_out, ffn2_pre_g, ffn2_post_g, ffn2_w_gate, ffn2_w_up, ffn2_w_down):
    nb, seq, _ = x_prompt.shape
    db, dseq, _ = x_sample.shape
    rows_p, rows_s = nb * seq, db * dseq

    conv_s = jnp.pad(state_conv, ((0, 0), (0, 0), (CONV_HIST - (CONV_K - 1), 0), (0, 0)))
    pool_s = jnp.pad(state_pool, ((0, 0), (0, 0), (POOL_HIST - POOL_BUF, 0), (0, 0)))

    gain = lambda g: g[:, None, :]
    ffn1 = (gain(ffn1_pre_g), gain(ffn1_post_g), ffn1_w_gate.astype(BF16),
            ffn1_w_up.astype(BF16), ffn1_w_down.astype(BF16))
    ffn2 = (gain(ffn2_pre_g), gain(ffn2_post_g), ffn2_w_gate.astype(BF16),
            ffn2_w_up.astype(BF16), ffn2_w_down.astype(BF16))
    mix = (gain(mix_pre_g), gain(mix_post_g), w_mix_in.astype(BF16), w_mix_out.astype(BF16),
           jnp.pad(conv_w, ((0, 0), (0, SUBLANES - CONV_K), (0, 0))),
           _block_diag_pool_weight(pool_w).astype(BF16), gain(pool_scale))

    prompt_states, sample_states = [], []
    ret_acc = None
    stream = (x_prompt.reshape(rows_p, D_MODEL), x_sample.reshape(rows_s, D_MODEL))
    for l in range(DEPTH):
        stream = _ffn(stream, l, ffn1, rows_p, rows_s, "join" if l == 0 else "stream")
        stream, *pst = _prompt_mixer(stream, l, mix, nb, seq)
        prompt_states.append(pst)
        stream, conv_l, ret_acc, pool_l = _sample_mixer(
            stream, l, mix, conv_s, state_ret, pool_s, ret_acc, rows_p, db, dseq, PAST_LEN)
        sample_states.append((conv_l, pool_l))
        stream = _ffn((stream,), l, ffn2, rows_p, rows_s, "fork" if l == DEPTH - 1 else "stream")
        if l < DEPTH - 1:
            stream = (stream,)
    y_prompt, y_sample = stream

    stack = lambda states, i: jnp.stack([s[i] for s in states])
    conv_tail = slice(CONV_HIST - (CONV_K - 1), None)
    pool_tail = slice(POOL_HIST - POOL_BUF, None)
    return (y_prompt.reshape(nb, seq, D_MODEL), y_sample.reshape(db, dseq, D_MODEL),
            stack(prompt_states, 0)[:, :, conv_tail], stack(prompt_states, 1),
            stack(prompt_states, 2)[:, :, pool_tail],
            stack(sample_states, 0)[:, :, conv_tail], ret_acc,
            stack(sample_states, 1)[:, :, pool_tail])
```

```python
import functools

import numpy as np
import jax
import jax.numpy as jnp
from jax import lax
from jax.experimental import pallas as pl
from jax.experimental.pallas import tpu as pltpu

D_MODEL = 1024
DEPTH = 4
PAST_LEN = 16384
CONV_DIM = D_MODEL // 4
RET_DIM = D_MODEL // 2
POOL_DIM = D_MODEL // 4
RET_HEADS = 4
RET_HEAD_DIM = RET_DIM // RET_HEADS
ROPE_BASE = 10000.0
CONV_K = 3
POOL_WINDOWS = (2, 4, 8, 16)
POOL_GROUP_DIM = POOL_DIM // len(POOL_WINDOWS)
POOL_BUF = max(POOL_WINDOWS) - 1
D_FF = 2816
EPS = 1e-6
MIX_IN_DIM = 3 * CONV_DIM + 4 * RET_DIM + POOL_DIM

SUBLANES = 8
BF16_SUBLANES = 16
VMEM_LIMIT_BYTES = 56 * 1024 * 1024

CONV_HIST = SUBLANES
POOL_HIST = 2 * SUBLANES

FFN_ROWS = 1024
FFN_SUBTILES = 2
CAST_SLABS = 16
PROMPT_ROWS = 512
RET_CHUNK = 256
SAMPLE_SEQS = 16

BF16 = jnp.bfloat16
F32 = jnp.float32


def _rms(x, g):
    return x * lax.rsqrt(jnp.mean(x * x, axis=-1, keepdims=True) + EPS) * g


def _dot(a, b):
    return jnp.dot(a, b, preferred_element_type=F32)


def _const_spec(shape):
    zeros = (0,) * len(shape)
    return pl.BlockSpec(shape, lambda *_: zeros, pipeline_mode=pl.Buffered(1))


def _ffn_rows(x, pre_g_ref, post_g_ref, wg_ref, wu_ref, wd_ref):
    h = _rms(x, pre_g_ref[...]).astype(BF16)
    gate = _dot(h, wg_ref[...])
    up = _dot(h, wu_ref[...])
    act = (gate * jax.nn.sigmoid(gate) * up).astype(BF16)
    return x + 0.5 * _rms(_dot(act, wd_ref[...]), post_g_ref[...])


def _ffn_body(x, weight_refs, cast_in, cast_out):
    for src, dst in zip(cast_in, cast_out):
        dst[...] = src[...].astype(BF16)
    n = x.shape[0] // FFN_SUBTILES
    return jnp.concatenate(
        [_ffn_rows(x[j * n:(j + 1) * n], *weight_refs) for j in range(FFN_SUBTILES)], axis=0)


def _ffn_kernel(mode, split, n_cast, *refs):
    n_x = 2 if mode == "join" else 1
    n_o = 2 if mode == "fork" else 1
    x_refs, refs = refs[:n_x], refs[n_x:]
    weight_refs, refs = refs[:5], refs[5:]
    cast_in, refs = refs[:n_cast], refs[n_cast:]
    o_refs, cast_out = refs[:n_o], refs[n_o:]
    if mode == "stream":
        o_refs[0][...] = _ffn_body(x_refs[0][...], weight_refs, cast_in, cast_out)
        return
    i = pl.program_id(0)

    @pl.when(i < split)
    def _():
        o_refs[0][...] = _ffn_body(x_refs[0][...], weight_refs, cast_in, cast_out)

    @pl.when(i >= split)
    def _():
        o_refs[-1][...] = _ffn_body(x_refs[-1][...], weight_refs, cast_in, cast_out)


def _ffn(xs, weights, cast_jobs, rows_a, rows_b, mode):
    tm = FFN_ROWS
    assert rows_a % tm == 0 and rows_b % tm == 0
    split = rows_a // tm
    steps = (rows_a + rows_b) // tm
    assert steps >= CAST_SLABS
    row_spec = pl.BlockSpec((tm, D_MODEL), lambda i: (i, 0))
    a_spec = pl.BlockSpec((tm, D_MODEL), lambda i: (jnp.minimum(i, split - 1), 0))
    b_spec = pl.BlockSpec((tm, D_MODEL), lambda i: (jnp.maximum(i - split, 0), 0))
    stream_shape = jax.ShapeDtypeStruct((rows_a + rows_b, D_MODEL), F32)
    weight_specs = [_const_spec(w.shape) for w in weights]
    if mode == "join":
        b_in_spec = pl.BlockSpec(b_spec.block_shape, b_spec.index_map, pipeline_mode=pl.Buffered(1))
        x_specs, out_specs, out_shape = [a_spec, b_in_spec], [row_spec], [stream_shape]
    elif mode == "fork":
        x_specs, out_specs = [row_spec], [a_spec, b_spec]
        out_shape = [jax.ShapeDtypeStruct((rows_a, D_MODEL), F32),
                     jax.ShapeDtypeStruct((rows_b, D_MODEL), F32)]
    else:
        x_specs, out_specs, out_shape = [row_spec], [row_spec], [stream_shape]
    slab_index = lambda i: jnp.minimum(i, CAST_SLABS - 1)
    cast_in_specs, cast_out_specs, cast_out_shape = [], [], []
    for w, layer in cast_jobs:
        _, r, c = w.shape
        slab = r // CAST_SLABS
        assert r % CAST_SLABS == 0 and slab % BF16_SUBLANES == 0
        cast_in_specs.append(pl.BlockSpec(
            (None, slab, c), lambda i, layer=layer: (layer, slab_index(i), 0)))
        cast_out_specs.append(pl.BlockSpec((slab, c), lambda i: (slab_index(i), 0)))
        cast_out_shape.append(jax.ShapeDtypeStruct((r, c), BF16))
    return pl.pallas_call(
        functools.partial(_ffn_kernel, mode, split, len(cast_jobs)),
        grid=(steps,),
        in_specs=x_specs + weight_specs + cast_in_specs,
        out_specs=out_specs + cast_out_specs,
        out_shape=out_shape + cast_out_shape,
        compiler_params=pltpu.CompilerParams(
            dimension_semantics=("arbitrary",), vmem_limit_bytes=VMEM_LIMIT_BYTES),
        name="ffn_" + mode,
    )(*xs, *weights, *[w for w, _ in cast_jobs])


def _conv_from_ext(ext, w_ref):
    w = w_ref[...]
    return (ext * w[2:3, :] + pltpu.roll(ext, 1, 0) * w[1:2, :]
            + pltpu.roll(ext, 2, 0) * w[0:1, :])


def _pool_sums_from_ext(ext):
    w2 = ext + pltpu.roll(ext, 1, 0)
    w4 = w2 + pltpu.roll(w2, 2, 0)
    w8 = w4 + pltpu.roll(w4, 4, 0)
    w16 = w8 + pltpu.roll(w8, 8, 0)
    lane = lax.broadcasted_iota(jnp.int32, ext.shape, 1)
    g = POOL_GROUP_DIM
    return jnp.where(lane < g, w2, jnp.where(lane < 2 * g, w4, jnp.where(lane < 3 * g, w8, w16)))


def _rope(x, cos, sin_signed):
    return x * cos + pltpu.roll(x, RET_HEAD_DIM // 2, 1) * sin_signed


def _group_norm_gate(o, gate):
    mu = jnp.mean(o, axis=-1, keepdims=True)
    c = o - mu
    var = jnp.mean(c * c, axis=-1, keepdims=True)
    return gate * jax.nn.sigmoid(gate) * (c * lax.rsqrt(var + EPS))


def _split_z(z):
    c = CONV_DIM
    r0 = 3 * c
    cx, cb, cc = z[:, 0:c], z[:, c:2 * c], z[:, 2 * c:3 * c]
    q = z[:, r0:r0 + RET_DIM]
    k = z[:, r0 + RET_DIM:r0 + 2 * RET_DIM]
    v = z[:, r0 + 2 * RET_DIM:r0 + 3 * RET_DIM]
    g = z[:, r0 + 3 * RET_DIM:r0 + 4 * RET_DIM]
    pu = z[:, r0 + 4 * RET_DIM:]
    return cx, cb, cc, q, k, v, g, pu


def _head(t, h):
    return t[:, h * RET_HEAD_DIM:(h + 1) * RET_HEAD_DIM]


def _prompt_mixer_kernel(chunk_decay,
                         x_ref, pre_g_ref, post_g_ref, w_in_ref, w_out_ref, conv_w_ref,
                         pool_w_ref, pool_scale_ref, cosq_ref, sinq_ref, cosk_ref, sink_ref,
                         mask_ref, cdec_ref, kdec_ref, cnt_ref,
                         o_ref, conv_o_ref, ret_o_ref, pool_o_ref,
                         conv_ext, pool_ext, s_ref):
    t = pl.program_id(1)
    rows = x_ref.shape[0]
    chunk = mask_ref.shape[1]

    @pl.when(t == 0)
    def _():
        conv_ext[0:CONV_HIST, :] = jnp.zeros((CONV_HIST, CONV_DIM), F32)
        pool_ext[0:POOL_HIST, :] = jnp.zeros((POOL_HIST, POOL_DIM), F32)
        s_ref[...] = jnp.zeros(s_ref.shape, F32)

    x = x_ref[...]
    h = _rms(x, pre_g_ref[...]).astype(BF16)
    z = _dot(h, w_in_ref[...])
    cx, cb, cc, q, k, v, g, pu = _split_z(z)

    conv_ext[CONV_HIST:, :] = cc * cx
    ext = conv_ext[...]
    yc = cb * _conv_from_ext(ext, conv_w_ref)[CONV_HIST:, :]
    conv_tail = ext[rows:, :]
    conv_ext[0:CONV_HIST, :] = conv_tail
    conv_o_ref[0] = conv_tail

    pool_ext[POOL_HIST:, :] = pu
    ext = pool_ext[...]
    mean = _pool_sums_from_ext(ext)[POOL_HIST:, :] / cnt_ref[...]
    yp = _dot((mean - pu).astype(BF16), pool_w_ref[...]) * pool_scale_ref[...]
    pool_tail = ext[rows:, :]
    pool_ext[0:POOL_HIST, :] = pool_tail
    pool_o_ref[0] = pool_tail

    outs = [yc]
    for hd in range(RET_HEADS):
        s = s_ref[hd]
        head_out = []
        for c in range(rows // chunk):
            sl = slice(c * chunk, (c + 1) * chunk)
            qh = _rope(_head(q, hd)[sl], cosq_ref[sl, :], sinq_ref[sl, :]).astype(BF16)
            kr = _rope(_head(k, hd)[sl], cosk_ref[sl, :], sink_ref[sl, :])
            kh = kr.astype(BF16)
            vh = _head(v, hd)[sl].astype(BF16)
            scores = lax.dot_general(qh, kh, (((1,), (1,)), ((), ())), preferred_element_type=F32)
            inner = _dot((scores * mask_ref[hd]).astype(BF16), vh)
            cross = _dot(qh, s.astype(BF16)) * cdec_ref[hd]
            kd = (kr * kdec_ref[hd]).astype(BF16)
            s = s * chunk_decay[hd] + lax.dot_general(
                kd, vh, (((0,), (0,)), ((), ())), preferred_element_type=F32)
            head_out.append(inner + cross)
        s_ref[hd] = s
        ret_o_ref[0, hd] = s
        outs.append(_group_norm_gate(jnp.concatenate(head_out, axis=0), _head(g, hd)))
    outs.append(yp)

    m = _dot(jnp.concatenate(outs, axis=-1).astype(BF16), w_out_ref[...])
    o_ref[...] = x + _rms(m, post_g_ref[...])


def _sample_mixer_kernel(chunk_decay, aliased_ret,
                         x_ref, pre_g_ref, post_g_ref, w_in_ref, w_out_ref, conv_w_ref,
                         pool_w_ref, pool_scale_ref, cosq_ref, sinq_ref, cosk_ref, sink_ref,
                         mask_ref, cdec_ref, kdec_ref, cnt_ref,
                         conv_s_ref, ret_s_ref, pool_s_ref, *rest):
    o_ref, conv_o_ref, ret_o_ref, pool_o_ref, conv_ext, pool_ext = rest[1 if aliased_ret else 0:]
    nseq, seq = conv_s_ref.shape[0], x_ref.shape[0] // conv_s_ref.shape[0]
    rows = nseq * seq

    x = x_ref[...]
    h = _rms(x, pre_g_ref[...]).astype(BF16)
    z = _dot(h, w_in_ref[...])
    cx, cb, cc, q, k, v, g, pu = _split_z(z)

    conv_ext[:, 0:CONV_HIST, :] = conv_s_ref[...]
    conv_ext[:, CONV_HIST:, :] = (cc * cx).reshape(nseq, seq, CONV_DIM)
    ext = conv_ext[...]
    y = _conv_from_ext(ext.reshape(nseq * (CONV_HIST + seq), CONV_DIM), conv_w_ref)
    y = y.reshape(nseq, CONV_HIST + seq, CONV_DIM)[:, CONV_HIST:, :]
    yc = cb * y.reshape(rows, CONV_DIM)
    conv_o_ref[...] = ext[:, seq:, :]

    pool_ext[:, 0:POOL_HIST, :] = pool_s_ref[...]
    pool_ext[:, POOL_HIST:, :] = pu.reshape(nseq, seq, POOL_DIM)
    ext = pool_ext[...]
    sums = _pool_sums_from_ext(ext.reshape(nseq * (POOL_HIST + seq), POOL_DIM))
    sums = sums.reshape(nseq, POOL_HIST + seq, POOL_DIM)[:, POOL_HIST:, :]
    mean = sums.reshape(rows, POOL_DIM) / cnt_ref[...]
    yp = _dot((mean - pu).astype(BF16), pool_w_ref[...]) * pool_scale_ref[...]
    pool_o_ref[...] = ext[:, seq:, :]

    outs = [yc]
    for hd in range(RET_HEADS):
        qh = _rope(_head(q, hd), cosq_ref[...], sinq_ref[...]).astype(BF16)
        kr = _rope(_head(k, hd), cosk_ref[...], sink_ref[...])
        kh = kr.astype(BF16)
        vh = _head(v, hd).astype(BF16)
        s_old = ret_s_ref[:, hd]
        scores = lax.dot_general(qh, kh, (((1,), (1,)), ((), ())), preferred_element_type=F32)
        inner = _dot((scores * mask_ref[hd]).astype(BF16), vh)
        q3 = qh.reshape(nseq, seq, RET_HEAD_DIM)
        cross = lax.dot_general(q3, s_old.astype(BF16), (((2,), (1,)), ((0,), (0,))),
                                preferred_element_type=F32)
        cross = cross.reshape(rows, RET_HEAD_DIM) * cdec_ref[hd]
        kd3 = (kr * kdec_ref[hd]).astype(BF16).reshape(nseq, seq, RET_HEAD_DIM)
        v3 = vh.reshape(nseq, seq, RET_HEAD_DIM)
        kv = lax.dot_general(kd3, v3, (((1,), (1,)), ((0,), (0,))), preferred_element_type=F32)
        ret_o_ref[:, hd] = s_old * chunk_decay[hd] + kv
        outs.append(_group_norm_gate(inner + cross, _head(g, hd)))
    outs.append(yp)

    m = _dot(jnp.concatenate(outs, axis=-1).astype(BF16), w_out_ref[...])
    o_ref[...] = x + _rms(m, post_g_ref[...])


def _log_gamma():
    return np.log(1.0 - 2.0 ** (-5.0 - np.arange(RET_HEADS, dtype=np.float64)))


def _rope_tables(pos):
    half = RET_HEAD_DIM // 2
    inv = ROPE_BASE ** (-np.arange(half, dtype=np.float64) / half)
    ang = pos.astype(np.float64)[:, None] * inv[None, :]
    cos, sin = np.cos(ang), np.sin(ang)
    return np.concatenate([cos, cos], -1), np.concatenate([-sin, sin], -1)


def _retention_tables(seq_of_row, idx_of_row, chunk):
    lg = _log_gamma()[:, None, None]
    diff = (idx_of_row[:, None] - idx_of_row[None, :]).astype(np.float64)
    same = seq_of_row[:, None] == seq_of_row[None, :]
    mask = np.where(same & (diff >= 0), np.exp(np.maximum(diff, 0.0)[None] * lg), 0.0)
    ones = np.ones((1, 1, RET_HEAD_DIM))
    cdec = np.exp((idx_of_row + 1.0)[None, :, None] * lg) * ones
    kdec = np.exp((chunk - 1.0 - idx_of_row)[None, :, None] * lg) * ones
    chunk_decay = tuple(float(d) for d in np.exp(chunk * _log_gamma()))
    return mask, cdec, kdec, chunk_decay


def _pool_counts(pos):
    win = np.repeat(np.asarray(POOL_WINDOWS, np.float64), POOL_GROUP_DIM)
    return np.minimum(pos.astype(np.float64)[:, None] + 1.0, win[None, :])


def _f32(a):
    return jnp.asarray(np.asarray(a, np.float32))


def _mixer_weight_specs(weights):
    return [_const_spec(w.shape) for w in weights]


def _prompt_mixer(stream, weights, batch, seq):
    rows, c = PROMPT_ROWS, RET_CHUNK
    assert seq % rows == 0 and rows % c == 0
    steps = seq // rows
    pos = np.arange(seq)
    cos, sin = _rope_tables(pos)
    scale = RET_HEAD_DIM ** -0.5
    mask, cdec, kdec, chunk_decay = _retention_tables(np.zeros(c, np.int64), np.arange(c), c)
    cnt = _pool_counts(pos)

    def pos_spec(width):
        return pl.BlockSpec((rows, width), lambda b, t: (t, 0))

    def state_spec(*shape):
        zeros = (0,) * len(shape)
        return pl.BlockSpec((1,) + shape, lambda b, t: (b,) + zeros)

    x_spec = pl.BlockSpec((rows, D_MODEL), lambda b, t: (b * steps + t, 0))
    return pl.pallas_call(
        functools.partial(_prompt_mixer_kernel, chunk_decay),
        grid=(batch, steps),
        in_specs=[x_spec] + _mixer_weight_specs(weights) + [
            pos_spec(RET_HEAD_DIM), pos_spec(RET_HEAD_DIM), pos_spec(RET_HEAD_DIM),
            pos_spec(RET_HEAD_DIM),
            _const_spec((RET_HEADS, c, c)),
            _const_spec((RET_HEADS, c, RET_HEAD_DIM)),
            _const_spec((RET_HEADS, c, RET_HEAD_DIM)),
            pos_spec(POOL_DIM),
        ],
        out_specs=[
            x_spec,
            state_spec(CONV_HIST, CONV_DIM),
            state_spec(RET_HEADS, RET_HEAD_DIM, RET_HEAD_DIM),
            state_spec(POOL_HIST, POOL_DIM),
        ],
        out_shape=[
            jax.ShapeDtypeStruct(stream.shape, F32),
            jax.ShapeDtypeStruct((batch, CONV_HIST, CONV_DIM), F32),
            jax.ShapeDtypeStruct((batch, RET_HEADS, RET_HEAD_DIM, RET_HEAD_DIM), F32),
            jax.ShapeDtypeStruct((batch, POOL_HIST, POOL_DIM), F32),
        ],
        scratch_shapes=[
            pltpu.VMEM((CONV_HIST + rows, CONV_DIM), F32),
            pltpu.VMEM((POOL_HIST + rows, POOL_DIM), F32),
            pltpu.VMEM((RET_HEADS, RET_HEAD_DIM, RET_HEAD_DIM), F32),
        ],
        input_output_aliases={0: 0},
        compiler_params=pltpu.CompilerParams(
            dimension_semantics=("arbitrary", "arbitrary"), vmem_limit_bytes=VMEM_LIMIT_BYTES),
        name="prompt_mixer",
    )(stream, *weights, _f32(cos * scale), _f32(sin * scale), _f32(cos), _f32(sin),
      _f32(mask), _f32(cdec), _f32(kdec), _f32(cnt))


def _sample_mixer(stream, layer, weights, conv_s, ret_s, pool_s, ret_acc, row0, batch, seq, pos0):
    n = SAMPLE_SEQS
    rows = n * seq
    assert seq == SUBLANES and batch % n == 0 and row0 % rows == 0
    seq_of_row = np.repeat(np.arange(n), seq)
    idx_of_row = np.tile(np.arange(seq), n)
    cos, sin = _rope_tables(pos0 + idx_of_row)
    scale = RET_HEAD_DIM ** -0.5
    mask, cdec, kdec, chunk_decay = _retention_tables(seq_of_row, idx_of_row, seq)
    cnt = _pool_counts(pos0 + idx_of_row)

    def state_spec(*shape):
        zeros = (0,) * len(shape)
        return pl.BlockSpec((None, n) + shape, lambda b: (layer, b) + zeros)

    x_spec = pl.BlockSpec((rows, D_MODEL), lambda b: (row0 // rows + b, 0))
    conv_spec = state_spec(CONV_HIST, CONV_DIM)
    ret_spec = state_spec(RET_HEADS, RET_HEAD_DIM, RET_HEAD_DIM)
    pool_spec = state_spec(POOL_HIST, POOL_DIM)
    in_specs = [x_spec] + _mixer_weight_specs(weights) + [
        _const_spec((rows, RET_HEAD_DIM)), _const_spec((rows, RET_HEAD_DIM)),
        _const_spec((rows, RET_HEAD_DIM)), _const_spec((rows, RET_HEAD_DIM)),
        _const_spec((RET_HEADS, rows, rows)),
        _const_spec((RET_HEADS, rows, RET_HEAD_DIM)),
        _const_spec((RET_HEADS, rows, RET_HEAD_DIM)),
        _const_spec((rows, POOL_DIM)),
        conv_spec, ret_spec, pool_spec,
    ]
    args = [stream, *weights, _f32(cos * scale), _f32(sin * scale), _f32(cos), _f32(sin),
            _f32(mask), _f32(cdec), _f32(kdec), _f32(cnt), conv_s, ret_s, pool_s]
    aliases = {0: 0}
    if ret_acc is not None:
        in_specs.append(pl.BlockSpec(memory_space=pl.ANY))
        aliases[len(args)] = 2
        args.append(ret_acc)
    return pl.pallas_call(
        functools.partial(_sample_mixer_kernel, chunk_decay, ret_acc is not None),
        grid=(batch // n,),
        in_specs=in_specs,
        out_specs=[
            x_spec,
            pl.BlockSpec((n, CONV_HIST, CONV_DIM), lambda b: (b, 0, 0)),
            ret_spec,
            pl.BlockSpec((n, POOL_HIST, POOL_DIM), lambda b: (b, 0, 0)),
        ],
        out_shape=[
            jax.ShapeDtypeStruct(stream.shape, F32),
            jax.ShapeDtypeStruct((batch, CONV_HIST, CONV_DIM), F32),
            jax.ShapeDtypeStruct(ret_s.shape, F32),
            jax.ShapeDtypeStruct((batch, POOL_HIST, POOL_DIM), F32),
        ],
        scratch_shapes=[
            pltpu.VMEM((n, CONV_HIST + seq, CONV_DIM), F32),
            pltpu.VMEM((n, POOL_HIST + seq, POOL_DIM), F32),
        ],
        input_output_aliases=aliases,
        compiler_params=pltpu.CompilerParams(
            dimension_semantics=("arbitrary",), vmem_limit_bytes=VMEM_LIMIT_BYTES),
        name="sample_mixer",
    )(*args)


def _block_diag_pool_weight(w):
    depth, groups, c, _ = w.shape
    eye = jnp.eye(groups, dtype=w.dtype)
    return (eye[None, :, None, :, None] * w[:, :, :, None, :]).reshape(depth, groups * c, groups * c)


def kernel(x_prompt, x_sample, state_conv, state_ret, state_pool, ffn1_pre_g, ffn1_post_g, ffn1_w_gate, ffn1_w_up, ffn1_w_down, mix_pre_g, mix_post_g, w_mix_in, conv_w, pool_w, pool_scale, w_mix_out, ffn2_pre_g, ffn2_post_g, ffn2_w_gate, ffn2_w_up, ffn2_w_down):
    nb, seq, _ = x_prompt.shape
    db, dseq, _ = x_sample.shape
    rows_p, rows_s = nb * seq, db * dseq

    conv_s = jnp.pad(state_conv, ((0, 0), (0, 0), (CONV_HIST - (CONV_K - 1), 0), (0, 0)))
    pool_s = jnp.pad(state_pool, ((0, 0), (0, 0), (POOL_HIST - POOL_BUF, 0), (0, 0)))

    conv_taps = jnp.pad(conv_w, ((0, 0), (0, SUBLANES - CONV_K), (0, 0)))
    pool_bd = _block_diag_pool_weight(pool_w).astype(BF16)
    row = lambda g, l: g[l][None, :]

    ffn1_w = [w[0].astype(BF16) for w in (ffn1_w_gate, ffn1_w_up, ffn1_w_down)]
    mix_w = [w[0].astype(BF16) for w in (w_mix_in, w_mix_out)]

    prompt_states, sample_states = [], []
    ret_acc = None
    stream = (x_prompt.reshape(rows_p, D_MODEL), x_sample.reshape(rows_s, D_MODEL))
    for l in range(DEPTH):
        last = l == DEPTH - 1
        stream, *ffn2_w = _ffn(
            stream, (row(ffn1_pre_g, l), row(ffn1_post_g, l), *ffn1_w),
            [(ffn2_w_gate, l), (ffn2_w_up, l), (ffn2_w_down, l)],
            rows_p, rows_s, "join" if l == 0 else "stream")
        mix_l = (row(mix_pre_g, l), row(mix_post_g, l), *mix_w, conv_taps[l], pool_bd[l],
                 row(pool_scale, l))
        stream, *pst = _prompt_mixer(stream, mix_l, nb, seq)
        prompt_states.append(pst)
        stream, conv_l, ret_acc, pool_l = _sample_mixer(
            stream, l, mix_l, conv_s, state_ret, pool_s, ret_acc, rows_p, db, dseq, PAST_LEN)
        sample_states.append((conv_l, pool_l))
        next_jobs = [] if last else [(w, l + 1) for w in (ffn1_w_gate, ffn1_w_up, ffn1_w_down,
                                                          w_mix_in, w_mix_out)]
        outs = _ffn((stream,), (row(ffn2_pre_g, l), row(ffn2_post_g, l), *ffn2_w), next_jobs,
                    rows_p, rows_s, "fork" if last else "stream")
        if last:
            y_prompt, y_sample = outs
        else:
            stream, ffn1_w, mix_w = (outs[0],), outs[1:4], outs[4:6]

    stack = lambda states, i: jnp.stack([s[i] for s in states])
    conv_tail = slice(CONV_HIST - (CONV_K - 1), None)
    pool_tail = slice(POOL_HIST - POOL_BUF, None)
    return (y_prompt.reshape(nb, seq, D_MODEL), y_sample.reshape(db, dseq, D_MODEL),
            stack(prompt_states, 0)[:, :, conv_tail], stack(prompt_states, 1),
            stack(prompt_states, 2)[:, :, pool_tail],
            stack(sample_states, 0)[:, :, conv_tail], ret_acc,
            stack(sample_states, 1)[:, :, pool_tail])
```

```python
import functools

import numpy as np
import jax
import jax.numpy as jnp
from jax import lax
from jax.experimental import pallas as pl
from jax.experimental.pallas import tpu as pltpu

D_MODEL = 1024
DEPTH = 4
PAST_LEN = 16384
CONV_DIM = D_MODEL // 4
RET_DIM = D_MODEL // 2
POOL_DIM = D_MODEL // 4
RET_HEADS = 4
RET_HEAD_DIM = RET_DIM // RET_HEADS
ROPE_BASE = 10000.0
CONV_K = 3
POOL_WINDOWS = (2, 4, 8, 16)
POOL_GROUP_DIM = POOL_DIM // len(POOL_WINDOWS)
POOL_BUF = max(POOL_WINDOWS) - 1
D_FF = 2816
EPS = 1e-6
MIX_IN_DIM = 3 * CONV_DIM + 4 * RET_DIM + POOL_DIM

SUBLANES = 8
BF16_SUBLANES = 16
VMEM_LIMIT_BYTES = 60 * 1024 * 1024

CONV_HIST = SUBLANES
POOL_HIST = 2 * SUBLANES

FFN_ROWS = 1024
FFN_SUBTILES = 2
CAST_SLABS = 16
PROMPT_ROWS = 512
PROMPT_GROUPS = 2
RET_CHUNK = 256
SAMPLE_SEQS = 32

BF16 = jnp.bfloat16
F32 = jnp.float32


def _rms(x, g):
    return x * lax.rsqrt(jnp.mean(x * x, axis=-1, keepdims=True) + EPS) * g


def _dot(a, b):
    return jnp.dot(a, b, preferred_element_type=F32)


def _const_spec(shape):
    zeros = (0,) * len(shape)
    return pl.BlockSpec(shape, lambda *_: zeros, pipeline_mode=pl.Buffered(1))


def _ffn_rows(x, pre_g_ref, post_g_ref, wg_ref, wu_ref, wd_ref):
    h = _rms(x, pre_g_ref[...]).astype(BF16)
    gate = _dot(h, wg_ref[...])
    up = _dot(h, wu_ref[...])
    act = (gate * jax.nn.sigmoid(gate) * up).astype(BF16)
    return x + 0.5 * _rms(_dot(act, wd_ref[...]), post_g_ref[...])


def _ffn_body(x, weight_refs, cast_in, cast_out):
    for src, dst in zip(cast_in, cast_out):
        dst[...] = src[...].astype(BF16)
    n = x.shape[0] // FFN_SUBTILES
    return jnp.concatenate(
        [_ffn_rows(x[j * n:(j + 1) * n], *weight_refs) for j in range(FFN_SUBTILES)], axis=0)


def _ffn_kernel(mode, split, n_cast, *refs):
    n_x = 2 if mode == "join" else 1
    n_o = 2 if mode == "fork" else 1
    x_refs, refs = refs[:n_x], refs[n_x:]
    weight_refs, refs = refs[:5], refs[5:]
    cast_in, refs = refs[:n_cast], refs[n_cast:]
    o_refs, cast_out = refs[:n_o], refs[n_o:]
    if mode == "stream":
        o_refs[0][...] = _ffn_body(x_refs[0][...], weight_refs, cast_in, cast_out)
        return
    i = pl.program_id(0)
    if mode == "join":
        x = jnp.where(i < split, x_refs[0][...], x_refs[1][...])
        o_refs[0][...] = _ffn_body(x, weight_refs, cast_in, cast_out)
        return

    @pl.when(i < split)
    def _():
        o_refs[0][...] = _ffn_body(x_refs[0][...], weight_refs, cast_in, cast_out)

    @pl.when(i >= split)
    def _():
        o_refs[1][...] = _ffn_body(x_refs[0][...], weight_refs, cast_in, cast_out)


def _ffn(xs, weights, cast_jobs, rows_a, rows_b, mode):
    tm = FFN_ROWS
    assert rows_a % tm == 0 and rows_b % tm == 0
    split = rows_a // tm
    steps = (rows_a + rows_b) // tm
    assert steps >= CAST_SLABS
    row_spec = pl.BlockSpec((tm, D_MODEL), lambda i: (i, 0))
    a_spec = pl.BlockSpec((tm, D_MODEL), lambda i: (jnp.minimum(i, split - 1), 0))
    b_spec = pl.BlockSpec((tm, D_MODEL), lambda i: (jnp.maximum(i - split, 0), 0))
    stream_shape = jax.ShapeDtypeStruct((rows_a + rows_b, D_MODEL), F32)
    weight_specs = [_const_spec(w.shape) for w in weights]
    if mode == "join":
        b_in_spec = pl.BlockSpec(b_spec.block_shape, b_spec.index_map, pipeline_mode=pl.Buffered(1))
        x_specs, out_specs, out_shape = [a_spec, b_in_spec], [row_spec], [stream_shape]
    elif mode == "fork":
        x_specs, out_specs = [row_spec], [a_spec, b_spec]
        out_shape = [jax.ShapeDtypeStruct((rows_a, D_MODEL), F32),
                     jax.ShapeDtypeStruct((rows_b, D_MODEL), F32)]
    else:
        x_specs, out_specs, out_shape = [row_spec], [row_spec], [stream_shape]
    slab_index = lambda i: jnp.minimum(i, CAST_SLABS - 1)
    cast_in_specs, cast_out_specs, cast_out_shape = [], [], []
    for w, layer in cast_jobs:
        _, r, c = w.shape
        slab = r // CAST_SLABS
        assert r % CAST_SLABS == 0 and slab % BF16_SUBLANES == 0
        cast_in_specs.append(pl.BlockSpec(
            (None, slab, c), lambda i, layer=layer: (layer, slab_index(i), 0)))
        cast_out_specs.append(pl.BlockSpec((slab, c), lambda i: (slab_index(i), 0)))
        cast_out_shape.append(jax.ShapeDtypeStruct((r, c), BF16))
    return pl.pallas_call(
        functools.partial(_ffn_kernel, mode, split, len(cast_jobs)),
        grid=(steps,),
        in_specs=x_specs + weight_specs + cast_in_specs,
        out_specs=out_specs + cast_out_specs,
        out_shape=out_shape + cast_out_shape,
        compiler_params=pltpu.CompilerParams(
            dimension_semantics=("arbitrary",), vmem_limit_bytes=VMEM_LIMIT_BYTES),
        name="ffn_" + mode,
    )(*xs, *weights, *[w for w, _ in cast_jobs])


def _conv_from_ext(ext, w_ref):
    w = w_ref[...]
    return (ext * w[2:3, :] + pltpu.roll(ext, 1, 0) * w[1:2, :]
            + pltpu.roll(ext, 2, 0) * w[0:1, :])


def _pool_sums_from_ext(ext):
    w2 = ext + pltpu.roll(ext, 1, 0)
    w4 = w2 + pltpu.roll(w2, 2, 0)
    w8 = w4 + pltpu.roll(w4, 4, 0)
    w16 = w8 + pltpu.roll(w8, 8, 0)
    lane = lax.broadcasted_iota(jnp.int32, ext.shape, 1)
    g = POOL_GROUP_DIM
    return jnp.where(lane < g, w2, jnp.where(lane < 2 * g, w4, jnp.where(lane < 3 * g, w8, w16)))


def _rope(x, cos, sin_signed):
    return x * cos + pltpu.roll(x, RET_HEAD_DIM // 2, 1) * sin_signed


def _group_norm_gate(o, gate):
    mu = jnp.mean(o, axis=-1, keepdims=True)
    c = o - mu
    var = jnp.mean(c * c, axis=-1, keepdims=True)
    return gate * jax.nn.sigmoid(gate) * (c * lax.rsqrt(var + EPS))


def _split_z(z):
    c = CONV_DIM
    r0 = 3 * c
    cx, cb, cc = z[:, 0:c], z[:, c:2 * c], z[:, 2 * c:3 * c]
    q = z[:, r0:r0 + RET_DIM]
    k = z[:, r0 + RET_DIM:r0 + 2 * RET_DIM]
    v = z[:, r0 + 2 * RET_DIM:r0 + 3 * RET_DIM]
    g = z[:, r0 + 3 * RET_DIM:r0 + 4 * RET_DIM]
    pu = z[:, r0 + 4 * RET_DIM:]
    return cx, cb, cc, q, k, v, g, pu


def _head(t, h):
    return t[:, h * RET_HEAD_DIM:(h + 1) * RET_HEAD_DIM]


def _prompt_mixer_kernel(chunk_decay,
                         x_ref, pre_g_ref, post_g_ref, w_in_ref, w_out_ref, conv_w_ref,
                         pool_w_ref, pool_scale_ref, cosq_ref, sinq_ref, cosk_ref, sink_ref,
                         mask_ref, cdec_ref, kdec_ref, cnt_ref,
                         o_ref, conv_o_ref, ret_o_ref, pool_o_ref,
                         conv_ext, pool_ext, s_ref):
    t = pl.program_id(1)
    rows = conv_ext.shape[0] - CONV_HIST
    chunk = mask_ref.shape[1]

    @pl.when(t == 0)
    def _():
        conv_ext[0:CONV_HIST, :] = jnp.zeros((CONV_HIST, CONV_DIM), F32)
        pool_ext[0:POOL_HIST, :] = jnp.zeros((POOL_HIST, POOL_DIM), F32)
        s_ref[...] = jnp.zeros(s_ref.shape, F32)

    for group in range(x_ref.shape[0] // rows):
        r0 = group * rows
        x = x_ref[r0:r0 + rows, :]
        h = _rms(x, pre_g_ref[...]).astype(BF16)
        z = _dot(h, w_in_ref[...])
        cx, cb, cc, q, k, v, g, pu = _split_z(z)

        conv_ext[CONV_HIST:, :] = cc * cx
        ext = conv_ext[...]
        yc = cb * _conv_from_ext(ext, conv_w_ref)[CONV_HIST:, :]
        conv_tail = ext[rows:, :]
        conv_ext[0:CONV_HIST, :] = conv_tail
        conv_o_ref[0] = conv_tail

        pool_ext[POOL_HIST:, :] = pu
        ext = pool_ext[...]
        mean = _pool_sums_from_ext(ext)[POOL_HIST:, :] / cnt_ref[r0:r0 + rows, :]
        yp = _dot((mean - pu).astype(BF16), pool_w_ref[...]) * pool_scale_ref[...]
        pool_tail = ext[rows:, :]
        pool_ext[0:POOL_HIST, :] = pool_tail
        pool_o_ref[0] = pool_tail

        outs = [yc]
        for hd in range(RET_HEADS):
            s = s_ref[hd]
            head_out = []
            for c in range(rows // chunk):
                sl = slice(c * chunk, (c + 1) * chunk)
                tl = slice(r0 + c * chunk, r0 + (c + 1) * chunk)
                qh = _rope(_head(q, hd)[sl], cosq_ref[tl, :], sinq_ref[tl, :]).astype(BF16)
                kr = _rope(_head(k, hd)[sl], cosk_ref[tl, :], sink_ref[tl, :])
                kh = kr.astype(BF16)
                vh = _head(v, hd)[sl].astype(BF16)
                scores = lax.dot_general(qh, kh, (((1,), (1,)), ((), ())),
                                         preferred_element_type=F32)
                inner = _dot((scores * mask_ref[hd]).astype(BF16), vh)
                cross = _dot(qh, s.astype(BF16)) * cdec_ref[hd]
                kd = (kr * kdec_ref[hd]).astype(BF16)
                s = s * chunk_decay[hd] + lax.dot_general(
                    kd, vh, (((0,), (0,)), ((), ())), preferred_element_type=F32)
                head_out.append(inner + cross)
            s_ref[hd] = s
            ret_o_ref[0, hd] = s
            outs.append(_group_norm_gate(jnp.concatenate(head_out, axis=0), _head(g, hd)))
        outs.append(yp)

        m = _dot(jnp.concatenate(outs, axis=-1).astype(BF16), w_out_ref[...])
        o_ref[r0:r0 + rows, :] = x + _rms(m, post_g_ref[...])


def _sample_mixer_kernel(chunk_decay, aliased_ret,
                         x_ref, pre_g_ref, post_g_ref, w_in_ref, w_out_ref, conv_w_ref,
                         pool_w_ref, pool_scale_ref, cosq_ref, sinq_ref, cosk_ref, sink_ref,
                         mask_ref, cdec_ref, kdec_ref, cnt_ref,
                         conv_s_ref, ret_s_ref, pool_s_ref, *rest):
    o_ref, conv_o_ref, ret_o_ref, pool_o_ref, conv_ext, pool_ext = rest[1 if aliased_ret else 0:]
    nseq, seq = conv_s_ref.shape[0], x_ref.shape[0] // conv_s_ref.shape[0]
    rows = nseq * seq

    x = x_ref[...]
    h = _rms(x, pre_g_ref[...]).astype(BF16)
    z = _dot(h, w_in_ref[...])
    cx, cb, cc, q, k, v, g, pu = _split_z(z)

    conv_ext[:, 0:CONV_HIST, :] = conv_s_ref[...]
    conv_ext[:, CONV_HIST:, :] = (cc * cx).reshape(nseq, seq, CONV_DIM)
    ext = conv_ext[...]
    y = _conv_from_ext(ext.reshape(nseq * (CONV_HIST + seq), CONV_DIM), conv_w_ref)
    y = y.reshape(nseq, CONV_HIST + seq, CONV_DIM)[:, CONV_HIST:, :]
    yc = cb * y.reshape(rows, CONV_DIM)
    conv_o_ref[...] = ext[:, seq:, :]

    pool_ext[:, 0:POOL_HIST, :] = pool_s_ref[...]
    pool_ext[:, POOL_HIST:, :] = pu.reshape(nseq, seq, POOL_DIM)
    ext = pool_ext[...]
    sums = _pool_sums_from_ext(ext.reshape(nseq * (POOL_HIST + seq), POOL_DIM))
    sums = sums.reshape(nseq, POOL_HIST + seq, POOL_DIM)[:, POOL_HIST:, :]
    mean = sums.reshape(rows, POOL_DIM) / cnt_ref[...]
    yp = _dot((mean - pu).astype(BF16), pool_w_ref[...]) * pool_scale_ref[...]
    pool_o_ref[...] = ext[:, seq:, :]

    outs = [yc]
    for hd in range(RET_HEADS):
        qh = _rope(_head(q, hd), cosq_ref[...], sinq_ref[...]).astype(BF16)
        kr = _rope(_head(k, hd), cosk_ref[...], sink_ref[...])
        kh = kr.astype(BF16)
        vh = _head(v, hd).astype(BF16)
        s_old = ret_s_ref[:, hd]
        scores = lax.dot_general(qh, kh, (((1,), (1,)), ((), ())), preferred_element_type=F32)
        inner = _dot((scores * mask_ref[hd]).astype(BF16), vh)
        q3 = qh.reshape(nseq, seq, RET_HEAD_DIM)
        cross = lax.dot_general(q3, s_old.astype(BF16), (((2,), (1,)), ((0,), (0,))),
                                preferred_element_type=F32)
        cross = cross.reshape(rows, RET_HEAD_DIM) * cdec_ref[hd]
        kd3 = (kr * kdec_ref[hd]).astype(BF16).reshape(nseq, seq, RET_HEAD_DIM)
        v3 = vh.reshape(nseq, seq, RET_HEAD_DIM)
        kv = lax.dot_general(kd3, v3, (((1,), (1,)), ((0,), (0,))), preferred_element_type=F32)
        ret_o_ref[:, hd] = s_old * chunk_decay[hd] + kv
        outs.append(_group_norm_gate(inner + cross, _head(g, hd)))
    outs.append(yp)

    m = _dot(jnp.concatenate(outs, axis=-1).astype(BF16), w_out_ref[...])
    o_ref[...] = x + _rms(m, post_g_ref[...])


def _log_gamma():
    return np.log(1.0 - 2.0 ** (-5.0 - np.arange(RET_HEADS, dtype=np.float64)))


def _rope_tables(pos):
    half = RET_HEAD_DIM // 2
    inv = ROPE_BASE ** (-np.arange(half, dtype=np.float64) / half)
    ang = pos.astype(np.float64)[:, None] * inv[None, :]
    cos, sin = np.cos(ang), np.sin(ang)
    return np.concatenate([cos, cos], -1), np.concatenate([-sin, sin], -1)


def _retention_tables(seq_of_row, idx_of_row, chunk):
    lg = _log_gamma()[:, None, None]
    diff = (idx_of_row[:, None] - idx_of_row[None, :]).astype(np.float64)
    same = seq_of_row[:, None] == seq_of_row[None, :]
    mask = np.where(same & (diff >= 0), np.exp(np.maximum(diff, 0.0)[None] * lg), 0.0)
    ones = np.ones((1, 1, RET_HEAD_DIM))
    cdec = np.exp((idx_of_row + 1.0)[None, :, None] * lg) * ones
    kdec = np.exp((chunk - 1.0 - idx_of_row)[None, :, None] * lg) * ones
    chunk_decay = tuple(float(d) for d in np.exp(chunk * _log_gamma()))
    return mask, cdec, kdec, chunk_decay


def _pool_counts(pos):
    win = np.repeat(np.asarray(POOL_WINDOWS, np.float64), POOL_GROUP_DIM)
    return np.minimum(pos.astype(np.float64)[:, None] + 1.0, win[None, :])


def _f32(a):
    return jnp.asarray(np.asarray(a, np.float32))


def _mixer_weight_specs(weights):
    return [_const_spec(w.shape) for w in weights]


def _prompt_mixer(stream, weights, batch, seq):
    group_rows, c = PROMPT_ROWS, RET_CHUNK
    rows = group_rows * PROMPT_GROUPS
    assert seq % rows == 0 and group_rows % c == 0
    steps = seq // rows
    pos = np.arange(seq)
    cos, sin = _rope_tables(pos)
    scale = RET_HEAD_DIM ** -0.5
    mask, cdec, kdec, chunk_decay = _retention_tables(np.zeros(c, np.int64), np.arange(c), c)
    cnt = _pool_counts(pos)

    def pos_spec(width):
        return pl.BlockSpec((rows, width), lambda b, t: (t, 0))

    def state_spec(*shape):
        zeros = (0,) * len(shape)
        return pl.BlockSpec((1,) + shape, lambda b, t: (b,) + zeros)

    x_spec = pl.BlockSpec((rows, D_MODEL), lambda b, t: (b * steps + t, 0))
    return pl.pallas_call(
        functools.partial(_prompt_mixer_kernel, chunk_decay),
        grid=(batch, steps),
        in_specs=[x_spec] + _mixer_weight_specs(weights) + [
            pos_spec(RET_HEAD_DIM), pos_spec(RET_HEAD_DIM), pos_spec(RET_HEAD_DIM),
            pos_spec(RET_HEAD_DIM),
            _const_spec((RET_HEADS, c, c)),
            _const_spec((RET_HEADS, c, RET_HEAD_DIM)),
            _const_spec((RET_HEADS, c, RET_HEAD_DIM)),
            pos_spec(POOL_DIM),
        ],
        out_specs=[
            x_spec,
            state_spec(CONV_HIST, CONV_DIM),
            state_spec(RET_HEADS, RET_HEAD_DIM, RET_HEAD_DIM),
            state_spec(POOL_HIST, POOL_DIM),
        ],
        out_shape=[
            jax.ShapeDtypeStruct(stream.shape, F32),
            jax.ShapeDtypeStruct((batch, CONV_HIST, CONV_DIM), F32),
            jax.ShapeDtypeStruct((batch, RET_HEADS, RET_HEAD_DIM, RET_HEAD_DIM), F32),
            jax.ShapeDtypeStruct((batch, POOL_HIST, POOL_DIM), F32),
        ],
        scratch_shapes=[
            pltpu.VMEM((CONV_HIST + group_rows, CONV_DIM), F32),
            pltpu.VMEM((POOL_HIST + group_rows, POOL_DIM), F32),
            pltpu.VMEM((RET_HEADS, RET_HEAD_DIM, RET_HEAD_DIM), F32),
        ],
        input_output_aliases={0: 0},
        compiler_params=pltpu.CompilerParams(
            dimension_semantics=("arbitrary", "arbitrary"), vmem_limit_bytes=VMEM_LIMIT_BYTES),
        name="prompt_mixer",
    )(stream, *weights, _f32(cos * scale), _f32(sin * scale), _f32(cos), _f32(sin),
      _f32(mask), _f32(cdec), _f32(kdec), _f32(cnt))


def _sample_mixer(stream, layer, weights, conv_s, ret_s, pool_s, ret_acc, row0, batch, seq, pos0):
    n = SAMPLE_SEQS
    rows = n * seq
    assert seq == SUBLANES and batch % n == 0 and row0 % rows == 0
    seq_of_row = np.repeat(np.arange(n), seq)
    idx_of_row = np.tile(np.arange(seq), n)
    cos, sin = _rope_tables(pos0 + idx_of_row)
    scale = RET_HEAD_DIM ** -0.5
    mask, cdec, kdec, chunk_decay = _retention_tables(seq_of_row, idx_of_row, seq)
    cnt = _pool_counts(pos0 + idx_of_row)

    def state_spec(*shape):
        zeros = (0,) * len(shape)
        return pl.BlockSpec((None, n) + shape, lambda b: (layer, b) + zeros)

    x_spec = pl.BlockSpec((rows, D_MODEL), lambda b: (row0 // rows + b, 0))
    conv_spec = state_spec(CONV_HIST, CONV_DIM)
    ret_spec = state_spec(RET_HEADS, RET_HEAD_DIM, RET_HEAD_DIM)
    pool_spec = state_spec(POOL_HIST, POOL_DIM)
    in_specs = [x_spec] + _mixer_weight_specs(weights) + [
        _const_spec((rows, RET_HEAD_DIM)), _const_spec((rows, RET_HEAD_DIM)),
        _const_spec((rows, RET_HEAD_DIM)), _const_spec((rows, RET_HEAD_DIM)),
        _const_spec((RET_HEADS, rows, rows)),
        _const_spec((RET_HEADS, rows, RET_HEAD_DIM)),
        _const_spec((RET_HEADS, rows, RET_HEAD_DIM)),
        _const_spec((rows, POOL_DIM)),
        conv_spec, ret_spec, pool_spec,
    ]
    args = [stream, *weights, _f32(cos * scale), _f32(sin * scale), _f32(cos), _f32(sin),
            _f32(mask), _f32(cdec), _f32(kdec), _f32(cnt), conv_s, ret_s, pool_s]
    aliases = {0: 0}
    if ret_acc is not None:
        in_specs.append(pl.BlockSpec(memory_space=pl.ANY))
        aliases[len(args)] = 2
        args.append(ret_acc)
    return pl.pallas_call(
        functools.partial(_sample_mixer_kernel, chunk_decay, ret_acc is not None),
        grid=(batch // n,),
        in_specs=in_specs,
        out_specs=[
            x_spec,
            pl.BlockSpec((n, CONV_HIST, CONV_DIM), lambda b: (b, 0, 0)),
            ret_spec,
            pl.BlockSpec((n, POOL_HIST, POOL_DIM), lambda b: (b, 0, 0)),
        ],
        out_shape=[
            jax.ShapeDtypeStruct(stream.shape, F32),
            jax.ShapeDtypeStruct((batch, CONV_HIST, CONV_DIM), F32),
            jax.ShapeDtypeStruct(ret_s.shape, F32),
            jax.ShapeDtypeStruct((batch, POOL_HIST, POOL_DIM), F32),
        ],
        scratch_shapes=[
            pltpu.VMEM((n, CONV_HIST + seq, CONV_DIM), F32),
            pltpu.VMEM((n, POOL_HIST + seq, POOL_DIM), F32),
        ],
        input_output_aliases=aliases,
        compiler_params=pltpu.CompilerParams(
            dimension_semantics=("arbitrary",), vmem_limit_bytes=VMEM_LIMIT_BYTES),
        name="sample_mixer",
    )(*args)


def _block_diag_pool_weight(w):
    depth, groups, c, _ = w.shape
    eye = jnp.eye(groups, dtype=w.dtype)
    return (eye[None, :, None, :, None] * w[:, :, :, None, :]).reshape(depth, groups * c, groups * c)


def kernel(x_prompt, x_sample, state_conv, state_ret, state_pool, ffn1_pre_g, ffn1_post_g, ffn1_w_gate, ffn1_w_up, ffn1_w_down, mix_pre_g, mix_post_g, w_mix_in, conv_w, pool_w, pool_scale, w_mix_out, ffn2_pre_g, ffn2_post_g, ffn2_w_gate, ffn2_w_up, ffn2_w_down):
    nb, seq, _ = x_prompt.shape
    db, dseq, _ = x_sample.shape
    rows_p, rows_s = nb * seq, db * dseq

    conv_s = jnp.pad(state_conv, ((0, 0), (0, 0), (CONV_HIST - (CONV_K - 1), 0), (0, 0)))
    pool_s = jnp.pad(state_pool, ((0, 0), (0, 0), (POOL_HIST - POOL_BUF, 0), (0, 0)))

    conv_taps = jnp.pad(conv_w, ((0, 0), (0, SUBLANES - CONV_K), (0, 0)))
    pool_bd = _block_diag_pool_weight(pool_w).astype(BF16)
    row = lambda g, l: g[l][None, :]

    ffn1_w = [w[0].astype(BF16) for w in (ffn1_w_gate, ffn1_w_up, ffn1_w_down)]
    mix_w = None

    prompt_states, sample_states = [], []
    ret_acc = None
    stream = (x_prompt.reshape(rows_p, D_MODEL), x_sample.reshape(rows_s, D_MODEL))
    for l in range(DEPTH):
        last = l == DEPTH - 1
        jobs = [(ffn2_w_gate, l), (ffn2_w_up, l), (ffn2_w_down, l)]
        if mix_w is None:
            jobs += [(w_mix_in, l), (w_mix_out, l)]
        stream, *cast = _ffn(stream, (row(ffn1_pre_g, l), row(ffn1_post_g, l), *ffn1_w), jobs,
                             rows_p, rows_s, "join" if l == 0 else "stream")
        ffn2_w, mix_w = cast[:3], cast[3:5] or mix_w
        mix_l = (row(mix_pre_g, l), row(mix_post_g, l), *mix_w, conv_taps[l], pool_bd[l],
                 row(pool_scale, l))
        stream, *pst = _prompt_mixer(stream, mix_l, nb, seq)
        prompt_states.append(pst)
        stream, conv_l, ret_acc, pool_l = _sample_mixer(
            stream, l, mix_l, conv_s, state_ret, pool_s, ret_acc, rows_p, db, dseq, PAST_LEN)
        sample_states.append((conv_l, pool_l))
        next_jobs = [] if last else [(w, l + 1) for w in (ffn1_w_gate, ffn1_w_up, ffn1_w_down,
                                                          w_mix_in, w_mix_out)]
        outs = _ffn((stream,), (row(ffn2_pre_g, l), row(ffn2_post_g, l), *ffn2_w), next_jobs,
                    rows_p, rows_s, "fork" if last else "stream")
        if last:
            y_prompt, y_sample = outs
        else:
            stream, ffn1_w, mix_w = (outs[0],), outs[1:4], outs[4:6]

    stack = lambda states, i: jnp.stack([s[i] for s in states])
    conv_tail = slice(CONV_HIST - (CONV_K - 1), None)
    pool_tail = slice(POOL_HIST - POOL_BUF, None)
    return (y_prompt.reshape(nb, seq, D_MODEL), y_sample.reshape(db, dseq, D_MODEL),
            stack(prompt_states, 0)[:, :, conv_tail], stack(prompt_states, 1),
            stack(prompt_states, 2)[:, :, pool_tail],
            stack(sample_states, 0)[:, :, conv_tail], ret_acc,
            stack(sample_states, 1)[:, :, pool_tail])
```
